```python
import jax, jax.numpy as jnp
from jax import lax
import numpy as np

D_MODEL = 1024
BATCH = 8
SEQ = 2048
DEPTH = 4
DEC_BATCH = 32
DEC_SEQ = 1
PAST_LEN = 8192
PAGE_SIZE = 128

HEAD_DIM = 128
N_HEADS = D_MODEL // HEAD_DIM
N_KV_HEADS = N_HEADS // 2
GROUP = N_HEADS // N_KV_HEADS
ATT_W = N_HEADS * HEAD_DIM
KV_W = N_KV_HEADS * HEAD_DIM
MOBA_BLOCK = 256
MOBA_TOPK = 3
Q_CHUNK = 8
SC_W = D_MODEL
SC_WIDTH = 3
LRU_W = D_MODEL
LRU_BLOCKS = 8
LRU_BW = LRU_W // LRU_BLOCKS
LRU_CONV = 4
LRU_C = 8.0
N_BRANCH = 3
MIX_W = D_MODEL
D_FF = 2816
FFN_CONV = 3
RMS_EPS = 1e-6
IN_SIZES = (ATT_W, KV_W, KV_W, SC_W, SC_W, SC_W, LRU_W, LRU_W)
N_IN = ATT_W + 2 * KV_W + 3 * SC_W + 2 * LRU_W

kernel_name = "hybrid_moba_shortconv_rglru_step"


def rms_norm(x, g):
    xf = x.astype(jnp.float32)
    y = xf * lax.rsqrt(jnp.mean(xf * xf, axis=-1, keepdims=True) + RMS_EPS)
    return (y * g.astype(jnp.float32)).astype(x.dtype)


def causal_dwconv(u, w, prev):
    K = w.shape[0]
    T = u.shape[1]
    ext = jnp.concatenate([prev.astype(u.dtype), u], axis=1)
    out = sum(w[j] * ext[:, j:j + T] for j in range(K))
    return out, ext[:, T:]


def _lin_combine(c1, c2):
    a1, b1 = c1
    a2, b2 = c2
    return a1 * a2, a2 * b1 + b2


def rg_lru(x, wa, ba, wx, bx, lam, h0):
    B, T, W = x.shape
    xb = x.reshape(B, T, LRU_BLOCKS, LRU_BW)
    r = jax.nn.sigmoid((jnp.einsum('btnd,nde->btne', xb, wa).reshape(B, T, W) + ba).astype(jnp.float32))
    i = jax.nn.sigmoid((jnp.einsum('btnd,nde->btne', xb, wx).reshape(B, T, W) + bx).astype(jnp.float32))
    log_a = -LRU_C * r * jax.nn.softplus(-lam.astype(jnp.float32))
    a = jnp.exp(log_a)
    b = jnp.sqrt(-jnp.expm1(2.0 * log_a)) * (i * x.astype(jnp.float32))
    b = b.at[:, 0].add(a[:, 0] * h0.astype(jnp.float32))
    _, h = lax.associative_scan(_lin_combine, (a, b), axis=1)
    return h.astype(x.dtype), h[:, -1]


def moba_attend(q, k_all, v_all, q_pos, q_chunk):
    B, Tq = q.shape[:2]
    L = k_all.shape[1]
    nb = -(-L // MOBA_BLOCK)
    pad = nb * MOBA_BLOCK - L
    k_pad = jnp.pad(k_all, ((0, 0), (0, pad), (0, 0), (0, 0)))
    v_pad = jnp.pad(v_all, ((0, 0), (0, pad), (0, 0), (0, 0)))
    k_blk = k_pad.reshape(B, nb, MOBA_BLOCK, N_KV_HEADS, HEAD_DIM)
    k_mean = jnp.mean(k_blk.astype(jnp.float32), axis=2)
    kb_t = jnp.transpose(k_blk, (0, 3, 1, 2, 4))
    vb_t = jnp.transpose(v_pad.reshape(B, nb, MOBA_BLOCK, N_KV_HEADS, HEAD_DIM), (0, 3, 1, 2, 4))
    n_sel = min(MOBA_TOPK, nb)
    scale = HEAD_DIM ** -0.5
    b_ix = jnp.arange(B)[:, None, None, None, None]
    h_ix = jnp.arange(N_KV_HEADS)[None, None, :, None, None]
    blk_ids = jnp.arange(nb)

    def one_chunk(args):
        qc, pc = args
        qblk = pc // MOBA_BLOCK
        gate = jnp.einsum('bqkgd,bnkd->bqkgn', qc.astype(jnp.float32), k_mean)
        past = blk_ids[None, :] < qblk[:, None]
        gate = jnp.where(past[None, :, None, None, :], gate, -jnp.inf)
        _, sel = lax.top_k(gate, n_sel)
        valid = jnp.arange(n_sel)[None, :] < qblk[:, None]
        ks = kb_t[b_ix, h_ix, sel]
        vs = vb_t[b_ix, h_ix, sel]
        s_sel = jnp.einsum('bqkgd,bqkgsrd->bqkgsr', qc, ks, preferred_element_type=jnp.float32) * scale
        s_sel = jnp.where(valid[None, :, None, None, :, None], s_sel, -jnp.inf)
        own_pos = qblk[:, None] * MOBA_BLOCK + jnp.arange(MOBA_BLOCK)[None, :]
        ko = k_pad[:, own_pos]
        vo = v_pad[:, own_pos]
        s_own = jnp.einsum('bqkgd,bqrkd->bqkgr', qc, ko, preferred_element_type=jnp.float32) * scale
        causal = own_pos <= pc[:, None]
        s_own = jnp.where(causal[None, :, None, None, :], s_own, -jnp.inf)
        s = jnp.concatenate([s_sel.reshape(B, q_chunk, N_KV_HEADS, GROUP, n_sel * MOBA_BLOCK), s_own], axis=-1)
        p = jax.nn.softmax(s, axis=-1).astype(v_all.dtype)
        p_sel = p[..., :n_sel * MOBA_BLOCK].reshape(B, q_chunk, N_KV_HEADS, GROUP, n_sel, MOBA_BLOCK)
        p_own = p[..., n_sel * MOBA_BLOCK:]
        return (jnp.einsum('bqkgsr,bqkgsrd->bqkgd', p_sel, vs)
                + jnp.einsum('bqkgr,bqrkd->bqkgd', p_own, vo))

    n_chunks = Tq // q_chunk
    qc = jnp.moveaxis(q.reshape(B, n_chunks, q_chunk, N_KV_HEADS, GROUP, HEAD_DIM), 1, 0)
    pc = q_pos.reshape(n_chunks, q_chunk)
    out = lax.map(one_chunk, (qc, pc))
    return jnp.moveaxis(out, 0, 1).reshape(B, Tq, ATT_W)


def trunk_layer(x, k_prev, v_prev, sconv_prev, rconv_prev, h0, fconv_prev, q_pos, q_chunk, lp):
    (g1, w_in, sconv_w, rconv_w, rconv_b, lru_wa, lru_ba, lru_wx, lru_bx, lru_lambda,
     w_branch, w_merge, b_merge, w_o, g2, w_up, fconv_w, fconv_b, w_down) = lp
    B, T, _ = x.shape
    xn = rms_norm(x, g1)
    proj = xn @ w_in
    q, k, v, b_gate, c_gate, h_sc, x_lru, g_lru = jnp.split(proj, list(np.cumsum(IN_SIZES)[:-1]), axis=-1)
    q = q.reshape(B, T, N_KV_HEADS, GROUP, HEAD_DIM)
    k = k.reshape(B, T, N_KV_HEADS, HEAD_DIM)
    v = v.reshape(B, T, N_KV_HEADS, HEAD_DIM)
    att = moba_attend(q, jnp.concatenate([k_prev, k], axis=1), jnp.concatenate([v_prev, v], axis=1), q_pos, q_chunk)
    sc, sconv_new = causal_dwconv(c_gate * h_sc, sconv_w, sconv_prev)
    o_sc = b_gate * sc
    xc, rconv_new = causal_dwconv(x_lru, rconv_w, rconv_prev)
    h, h_last = rg_lru(xc + rconv_b, lru_wa, lru_ba, lru_wx, lru_bx, lru_lambda, h0)
    o_lru = h * jax.nn.gelu(g_lru)
    branches = jnp.stack([att, o_sc, o_lru], axis=2)
    proj_br = jnp.einsum('btnw,nwd->btnd', branches, w_branch)
    gates = jax.nn.sigmoid(xn @ w_merge + b_merge).reshape(B, T, N_BRANCH, D_MODEL)
    x = x + jnp.sum(gates * proj_br, axis=2) @ w_o
    xn2 = rms_norm(x, g2)
    up, fconv_new = causal_dwconv(xn2 @ w_up, fconv_w, fconv_prev)
    gate, val = jnp.split(up + fconv_b, 2, axis=-1)
    x = x + (jax.nn.silu(gate) * val) @ w_down
    return x, (k, v, sconv_new, rconv_new, h_last, fconv_new)


def setup_inputs(seed: int = 0) -> dict:
    key = jax.random.key(seed)
    ks = jax.random.split(key, 40)
    f32 = jnp.float32
    n_pages = PAST_LEN // PAGE_SIZE
    n_pool = (DEC_BATCH * n_pages * 5) // 4

    def nrm(k, shape, scale):
        return jax.random.normal(k, shape, f32) * scale

    u = jax.random.uniform(ks[30], (DEPTH, LRU_W), f32, 0.9, 0.999)
    a0 = u ** (1.0 / LRU_C)
    lru_lambda = jnp.log(a0) - jnp.log1p(-a0)
    page_table = jax.random.permutation(ks[31], n_pool)[:DEC_BATCH * n_pages].reshape(DEC_BATCH, n_pages).astype(jnp.int32)
    return {
        'x_prompt': nrm(ks[0], (BATCH, SEQ, D_MODEL), 1.0),
        'x_sample': nrm(ks[1], (DEC_BATCH, DEC_SEQ, D_MODEL), 1.0),
        'cache_k': nrm(ks[2], (DEPTH, n_pool, PAGE_SIZE, N_KV_HEADS, HEAD_DIM), 1.0),
        'cache_v': nrm(ks[3], (DEPTH, n_pool, PAGE_SIZE, N_KV_HEADS, HEAD_DIM), 1.0),
        'state_sconv': nrm(ks[4], (DEPTH, DEC_BATCH, SC_WIDTH - 1, SC_W), 1.0),
        'state_rconv': nrm(ks[5], (DEPTH, DEC_BATCH, LRU_CONV - 1, LRU_W), 1.0),
        'state_lru': nrm(ks[6], (DEPTH, DEC_BATCH, LRU_W), 0.5),
        'state_fconv': nrm(ks[7], (DEPTH, DEC_BATCH, FFN_CONV - 1, 2 * D_FF), 1.0),
        'page_table': page_table,
        'norm1': 1.0 + nrm(ks[8], (DEPTH, D_MODEL), 0.01),
        'w_in': nrm(ks[9], (DEPTH, D_MODEL, N_IN), D_MODEL ** -0.5),
        'sconv_w': nrm(ks[10], (DEPTH, SC_WIDTH, SC_W), SC_WIDTH ** -0.5),
        'rconv_w': nrm(ks[11], (DEPTH, LRU_CONV, LRU_W), LRU_CONV ** -0.5),
        'rconv_b': nrm(ks[12], (DEPTH, LRU_W), 0.01),
        'lru_wa': nrm(ks[13], (DEPTH, LRU_BLOCKS, LRU_BW, LRU_BW), LRU_BW ** -0.5),
        'lru_ba': nrm(ks[14], (DEPTH, LRU_W), 0.01),
        'lru_wx': nrm(ks[15], (DEPTH, LRU_BLOCKS, LRU_BW, LRU_BW), LRU_BW ** -0.5),
        'lru_bx': nrm(ks[16], (DEPTH, LRU_W), 0.01),
        'lru_lambda': lru_lambda,
        'w_branch': nrm(ks[17], (DEPTH, N_BRANCH, MIX_W, D_MODEL), MIX_W ** -0.5),
        'w_merge': nrm(ks[18], (DEPTH, D_MODEL, N_BRANCH * D_MODEL), D_MODEL ** -0.5),
        'b_merge': nrm(ks[19], (DEPTH, N_BRANCH * D_MODEL), 0.01),
        'w_o': nrm(ks[20], (DEPTH, D_MODEL, D_MODEL), D_MODEL ** -0.5),
        'norm2': 1.0 + nrm(ks[21], (DEPTH, D_MODEL), 0.01),
        'w_up': nrm(ks[22], (DEPTH, D_MODEL, 2 * D_FF), D_MODEL ** -0.5),
        'fconv_w': nrm(ks[23], (DEPTH, FFN_CONV, 2 * D_FF), FFN_CONV ** -0.5),
        'fconv_b': nrm(ks[24], (DEPTH, 2 * D_FF), 0.01),
        'w_down': nrm(ks[25], (DEPTH, D_FF, D_MODEL), D_FF ** -0.5),
        'norm_f': 1.0 + nrm(ks[26], (D_MODEL,), 0.01),
    }


def reference(x_prompt, x_sample, cache_k, cache_v, state_sconv, state_rconv, state_lru, state_fconv,
              page_table, norm1, w_in, sconv_w, rconv_w, rconv_b, lru_wa, lru_ba, lru_wx, lru_bx,
              lru_lambda, w_branch, w_merge, b_merge, w_o, norm2, w_up, fconv_w, fconv_b, w_down, norm_f):
    B, T, _ = x_prompt.shape
    DB, DS, _ = x_sample.shape
    dt = x_prompt.dtype
    pos_p = jnp.arange(T, dtype=jnp.int32)
    pos_s = PAST_LEN + jnp.arange(DS, dtype=jnp.int32)
    xp, xs = x_prompt, x_sample
    st_p, st_s = [], []
    for l in range(DEPTH):
        lp = (norm1[l], w_in[l], sconv_w[l], rconv_w[l], rconv_b[l], lru_wa[l], lru_ba[l], lru_wx[l],
              lru_bx[l], lru_lambda[l], w_branch[l], w_merge[l], b_merge[l], w_o[l], norm2[l], w_up[l],
              fconv_w[l], fconv_b[l], w_down[l])
        xp, sp = trunk_layer(
            xp,
            jnp.zeros((B, 0, N_KV_HEADS, HEAD_DIM), dt), jnp.zeros((B, 0, N_KV_HEADS, HEAD_DIM), dt),
            jnp.zeros((B, SC_WIDTH - 1, SC_W), dt), jnp.zeros((B, LRU_CONV - 1, LRU_W), dt),
            jnp.zeros((B, LRU_W), jnp.float32), jnp.zeros((B, FFN_CONV - 1, 2 * D_FF), dt),
            pos_p, Q_CHUNK, lp)
        st_p.append(sp)
        k_past = cache_k[l][page_table].reshape(DB, -1, N_KV_HEADS, HEAD_DIM)
        v_past = cache_v[l][page_table].reshape(DB, -1, N_KV_HEADS, HEAD_DIM)
        xs, ss = trunk_layer(xs, k_past, v_past, state_sconv[l], state_rconv[l], state_lru[l],
                             state_fconv[l], pos_s, DS, lp)
        st_s.append(ss)
    y_prompt = rms_norm(xp, norm_f)
    y_sample = rms_norm(xs, norm_f)
    k_p, v_p, sconv_p, rconv_p, lru_p, fconv_p = [jnp.stack([s[i] for s in st_p], axis=0) for i in range(6)]
    k_s, v_s, sconv_s, rconv_s, lru_s, fconv_s = [jnp.stack([s[i] for s in st_s], axis=0) for i in range(6)]
    return (y_prompt, y_sample, k_p, v_p, sconv_p, rconv_p, lru_p, fconv_p,
            k_s, v_s, sconv_s, rconv_s, lru_s, fconv_s)
```

```python
import functools
import math

import jax
import jax.numpy as jnp
from jax import lax
from jax.experimental import pallas as pl
from jax.experimental.pallas import tpu as pltpu

F32 = jnp.float32
BF16 = jnp.bfloat16

HEAD_DIM = 128
N_HEADS = 8
N_KV_HEADS = 4
GROUP = N_HEADS // N_KV_HEADS
KV_W = N_KV_HEADS * HEAD_DIM
MOBA_BLOCK = 256
MOBA_TOPK = 3
LRU_BLOCKS = 8
LRU_C = 8.0
RMS_EPS = 1e-6
PAGE_SIZE = 128
PAGES_PER_BLOCK = MOBA_BLOCK // PAGE_SIZE
ATT_SCALE = HEAD_DIM ** -0.5

SUBLANES = 8
HALO = SUBLANES
VMEM_LIMIT = 56 * 1024 * 1024
TIME_TILE = 256
MEAN_PAGES = 16


def _dot(a, b):
    return jnp.dot(a, b, preferred_element_type=F32)


def _dot_nt(a, b, precision=None):
    return lax.dot_general(a, b, (((1,), (1,)), ((), ())), preferred_element_type=F32,
                           precision=precision)


def _rms(x, g):
    return x * lax.rsqrt(jnp.mean(x * x, axis=-1, keepdims=True) + RMS_EPS) * g


def _gelu_tanh(x):
    c = math.sqrt(2.0 / math.pi)
    return x * (0.5 * (1.0 + jnp.tanh(c * (x + 0.044715 * (x * x * x)))))


def _softplus(x):
    return jnp.maximum(x, 0.0) + jnp.log1p(jnp.exp(-jnp.abs(x)))


def _lru_coeffs(xc, r_lin, i_lin, lam):
    r = jax.nn.sigmoid(r_lin)
    i = jax.nn.sigmoid(i_lin)
    log_a = -LRU_C * r * _softplus(-lam)
    a = jnp.exp(log_a)
    th = jnp.tanh(log_a)
    b = jnp.sqrt(-2.0 * th / (1.0 - th)) * (i * xc)
    return a, b


def _const_spec(shape, index):
    return pl.BlockSpec(shape, lambda *_: index, pipeline_mode=pl.Buffered(1))


def _mix_kernel(x_ref, g1_ref, win_ref, scw_ref, rcw_ref, rcb_ref, wa_ref, ba_ref, wx_ref, bx_ref,
                lam_ref, wb1_ref, wb2_ref, wm_ref, bm_ref,
                q_ref, k_ref, v_ref, km_ref, g0_ref, mp_ref, scn_ref, rcn_ref, hl_ref,
                ext_sc, ext_rc, lin_a, lin_i, h_carry, *, tm, d):
    t = pl.program_id(1)
    last = pl.num_programs(1) - 1

    @pl.when(t == 0)
    def _():
        ext_sc[0:HALO, :] = jnp.zeros((HALO, d), F32)
        ext_rc[0:HALO, :] = jnp.zeros((HALO, d), F32)
        h_carry[...] = jnp.zeros((1, d), F32)

    xn = _rms(x_ref[...], g1_ref[...]).astype(BF16)

    q_ref[...] = _dot(xn, win_ref[:, 0:d])
    k = _dot(xn, win_ref[:, d:d + KV_W])
    k_ref[...] = k
    for j in range(tm // MOBA_BLOCK):
        km_ref[j:j + 1, :] = jnp.mean(k[j * MOBA_BLOCK:(j + 1) * MOBA_BLOCK], axis=0, keepdims=True)
    v_ref[...] = _dot(xn, win_ref[:, d + KV_W:d + 2 * KV_W])
    c0 = d + 2 * KV_W

    u = _dot(xn, win_ref[:, c0 + d:c0 + 2 * d]) * _dot(xn, win_ref[:, c0 + 2 * d:c0 + 3 * d])
    ext_sc[HALO:HALO + tm, :] = u
    sc = (scw_ref[0:1, :] * ext_sc[HALO - 2:HALO - 2 + tm, :]
          + scw_ref[1:2, :] * ext_sc[HALO - 1:HALO - 1 + tm, :]
          + scw_ref[2:3, :] * u)
    o_sc = (_dot(xn, win_ref[:, c0:c0 + d]) * sc).astype(BF16)
    pb1 = _dot(o_sc, wb1_ref[...])

    ext_rc[HALO:HALO + tm, :] = _dot(xn, win_ref[:, c0 + 3 * d:c0 + 4 * d])
    xc = (rcw_ref[0:1, :] * ext_rc[HALO - 3:HALO - 3 + tm, :]
          + rcw_ref[1:2, :] * ext_rc[HALO - 2:HALO - 2 + tm, :]
          + rcw_ref[2:3, :] * ext_rc[HALO - 1:HALO - 1 + tm, :]
          + rcw_ref[3:4, :] * ext_rc[HALO:HALO + tm, :]) + rcb_ref[...]
    bw = d // LRU_BLOCKS
    xcb = xc.astype(BF16)
    for n in range(LRU_BLOCKS):
        lin_a[:, n * bw:(n + 1) * bw] = _dot(xcb[:, n * bw:(n + 1) * bw], wa_ref[n])
        lin_i[:, n * bw:(n + 1) * bw] = _dot(xcb[:, n * bw:(n + 1) * bw], wx_ref[n])
    a, b = _lru_coeffs(xc, lin_a[...] + ba_ref[...], lin_i[...] + bx_ref[...], lam_ref[...])
    row = lax.broadcasted_iota(jnp.int32, (tm, d), 0)
    b = jnp.where(row == 0, b + a * h_carry[...], b)
    s = 1
    while s < tm:
        keep = row >= s
        b = b + a * jnp.where(keep, pltpu.roll(b, s, 0), 0.0)
        a = a * jnp.where(keep, pltpu.roll(a, s, 0), 1.0)
        s *= 2
    h = b
    h_carry[...] = h[tm - 1:tm, :]
    o_lru = (h * _gelu_tanh(_dot(xn, win_ref[:, c0 + 4 * d:c0 + 5 * d]))).astype(BF16)
    pb2 = _dot(o_lru, wb2_ref[...])

    g0_ref[...] = jax.nn.sigmoid(_dot(xn, wm_ref[:, 0:d]) + bm_ref[:, 0:d])
    mp_ref[...] = (jax.nn.sigmoid(_dot(xn, wm_ref[:, d:2 * d]) + bm_ref[:, d:2 * d]) * pb1
                   + jax.nn.sigmoid(_dot(xn, wm_ref[:, 2 * d:3 * d]) + bm_ref[:, 2 * d:3 * d]) * pb2)

    @pl.when(t == last)
    def _():
        scn_ref[...] = ext_sc[HALO + tm - 2:HALO + tm, :]
        rcn_ref[...] = ext_rc[HALO + tm - 3:HALO + tm, :]
        hl_ref[...] = h_carry[...]

    ext_sc[HALO - 2:HALO, :] = ext_sc[HALO + tm - 2:HALO + tm, :]
    ext_rc[HALO - 3:HALO, :] = ext_rc[HALO + tm - 3:HALO + tm, :]


def _prompt_mix(l, x, w, tm):
    bsz, t, d = x.shape
    nt = t // tm
    nmb = tm // MOBA_BLOCK
    n_in = w['w_in'].shape[-1]
    row = lambda b, i: (b, i, 0)
    per_b = lambda b, i: (b, 0, 0)
    vec = lambda n: _const_spec((None, 1, n), (l, 0, 0))
    in_specs = [
        pl.BlockSpec((None, tm, d), row),
        vec(d),
        _const_spec((None, d, n_in), (l, 0, 0)),
        _const_spec((None, 3, d), (l, 0, 0)),
        _const_spec((None, 4, d), (l, 0, 0)),
        vec(d),
        _const_spec((None, LRU_BLOCKS, d // LRU_BLOCKS, d // LRU_BLOCKS), (l, 0, 0, 0)),
        vec(d),
        _const_spec((None, LRU_BLOCKS, d // LRU_BLOCKS, d // LRU_BLOCKS), (l, 0, 0, 0)),
        vec(d),
        vec(d),
        _const_spec((None, None, d, d), (l, 1, 0, 0)),
        _const_spec((None, None, d, d), (l, 2, 0, 0)),
        _const_spec((None, d, 3 * d), (l, 0, 0)),
        vec(3 * d),
    ]
    out_shape = [
        jax.ShapeDtypeStruct((bsz, t, d), F32),
        jax.ShapeDtypeStruct((bsz, t, KV_W), F32),
        jax.ShapeDtypeStruct((bsz, t, KV_W), F32),
        jax.ShapeDtypeStruct((bsz, nt, nmb, KV_W), F32),
        jax.ShapeDtypeStruct((bsz, t, d), F32),
        jax.ShapeDtypeStruct((bsz, t, d), F32),
        jax.ShapeDtypeStruct((bsz, 2, d), F32),
        jax.ShapeDtypeStruct((bsz, 3, d), F32),
        jax.ShapeDtypeStruct((bsz, 1, d), F32),
    ]
    out_specs = [
        pl.BlockSpec((None, tm, d), row),
        pl.BlockSpec((None, tm, KV_W), row),
        pl.BlockSpec((None, tm, KV_W), row),
        pl.BlockSpec((None, None, nmb, KV_W), lambda b, i: (b, i, 0, 0)),
        pl.BlockSpec((None, tm, d), row),
        pl.BlockSpec((None, tm, d), row),
        pl.BlockSpec((None, 2, d), per_b),
        pl.BlockSpec((None, 3, d), per_b),
        pl.BlockSpec((None, 1, d), per_b),
    ]
    scratch = [
        pltpu.VMEM((HALO + tm, d), F32),
        pltpu.VMEM((HALO + tm, d), F32),
        pltpu.VMEM((tm, d), F32),
        pltpu.VMEM((tm, d), F32),
        pltpu.VMEM((1, d), F32),
    ]
    return pl.pallas_call(
        functools.partial(_mix_kernel, tm=tm, d=d),
        grid=(bsz, nt), in_specs=in_specs, out_specs=out_specs, out_shape=out_shape,
        scratch_shapes=scratch, name=f"prompt_mix_{l}",
        compiler_params=pltpu.CompilerParams(
            dimension_semantics=("arbitrary", "arbitrary"), vmem_limit_bytes=VMEM_LIMIT),
    )(x, w['norm1'], w['w_in'], w['sconv_w'], w['rconv_w'], w['rconv_b'], w['lru_wa'], w['lru_ba'],
      w['lru_wx'], w['lru_bx'], w['lru_lambda'], w['w_branch'], w['w_branch'], w['w_merge'],
      w['b_merge'])


def _attn_kernel(q_ref, k_ref, v_ref, km_ref, o_ref, *, nb):
    i = pl.program_id(2)
    blk = MOBA_BLOCK
    km = km_ref[...]
    km_pad = jnp.concatenate([km, jnp.zeros((HEAD_DIM - nb, HEAD_DIM), F32)], axis=0)
    lane = lax.broadcasted_iota(jnp.int32, (blk, HEAD_DIM), 1)
    r_ix = lax.broadcasted_iota(jnp.int32, (blk, blk), 0)
    c_ix = lax.broadcasted_iota(jnp.int32, (blk, blk), 1)
    k_own = k_ref[pl.ds(pl.multiple_of(i * blk, blk), blk), :].astype(BF16)
    v_own = v_ref[pl.ds(pl.multiple_of(i * blk, blk), blk), :].astype(BF16)

    for g in range(GROUP):
        qf = q_ref[:, g * HEAD_DIM:(g + 1) * HEAD_DIM]
        qb = qf.astype(BF16)
        gate = _dot_nt(qf, km_pad, precision=lax.Precision.HIGHEST)
        gate = jnp.where(lane < i, gate, -jnp.inf)
        rank = jnp.zeros((blk, HEAD_DIM), jnp.int32)
        for j in range(nb):
            gj = jnp.sum(jnp.where(lane == j, gate, 0.0), axis=1, keepdims=True)
            gj = jnp.where(j < i, gj, -jnp.inf)
            rank = rank + jnp.where((gj > gate) | ((gj == gate) & (j < lane)), 1, 0)
        sel = jnp.where((lane < i) & (rank < MOBA_TOPK), 1.0, 0.0)

        s = _dot_nt(qb, k_own) * ATT_SCALE
        s = jnp.where(c_ix <= r_ix, s, -jnp.inf)
        m0 = jnp.max(s, axis=1, keepdims=True)
        p = jnp.exp(s - m0)
        l0 = jnp.sum(p, axis=1, keepdims=True)
        acc0 = _dot(p.astype(BF16), v_own)

        def body(j, carry):
            m, l, acc = carry
            start = pl.multiple_of(j * blk, blk)
            kj = k_ref[pl.ds(start, blk), :].astype(BF16)
            vj = v_ref[pl.ds(start, blk), :].astype(BF16)
            on = jnp.sum(jnp.where(lane == j, sel, 0.0), axis=1, keepdims=True) > 0.5
            sj = jnp.where(on, _dot_nt(qb, kj) * ATT_SCALE, -jnp.inf)
            m_new = jnp.maximum(m, jnp.max(sj, axis=1, keepdims=True))
            alpha = jnp.exp(m - m_new)
            pj = jnp.exp(sj - m_new)
            l = alpha * l + jnp.sum(pj, axis=1, keepdims=True)
            acc = alpha * acc + _dot(pj.astype(BF16), vj)
            return m_new, l, acc

        _, l_f, acc_f = lax.fori_loop(0, i, body, (m0, l0, acc0))
        o_ref[:, g * HEAD_DIM:(g + 1) * HEAD_DIM] = (acc_f / l_f).astype(o_ref.dtype)


def _prompt_attn(q, k, v, kmean):
    bsz, t, d = q.shape
    nb = t // MOBA_BLOCK
    gw = GROUP * HEAD_DIM
    return pl.pallas_call(
        functools.partial(_attn_kernel, nb=nb),
        grid=(bsz, N_KV_HEADS, nb),
        in_specs=[
            pl.BlockSpec((None, MOBA_BLOCK, gw), lambda b, h, i: (b, i, h)),
            pl.BlockSpec((None, t, HEAD_DIM), lambda b, h, i: (b, 0, h)),
            pl.BlockSpec((None, t, HEAD_DIM), lambda b, h, i: (b, 0, h)),
            pl.BlockSpec((None, nb, HEAD_DIM), lambda b, h, i: (b, 0, h)),
        ],
        out_specs=pl.BlockSpec((None, MOBA_BLOCK, gw), lambda b, h, i: (b, i, h)),
        out_shape=jax.ShapeDtypeStruct((bsz, t, d), BF16),
        name="prompt_attn",
        compiler_params=pltpu.CompilerParams(
            dimension_semantics=("arbitrary", "arbitrary", "arbitrary")),
    )(q, k, v, kmean)


def _post_kernel(x_ref, att_ref, g0_ref, mp_ref, wb0_ref, wo_ref, g2_ref, wup_ref, fcw_ref, fcb_ref,
                 wdn_ref, gf_ref, y_ref, fcn_ref, ext_fc, *, tm, d, dff, final):
    t = pl.program_id(1)
    last = pl.num_programs(1) - 1

    @pl.when(t == 0)
    def _():
        ext_fc[0:HALO, :] = jnp.zeros((HALO, 2 * dff), F32)

    merged = g0_ref[...] * _dot(att_ref[...], wb0_ref[...]) + mp_ref[...]
    x1 = x_ref[...] + _dot(merged.astype(BF16), wo_ref[...])
    xn2 = _rms(x1, g2_ref[...]).astype(BF16)
    ext_fc[HALO:HALO + tm, :] = _dot(xn2, wup_ref[...])

    def conv(c0, c1):
        return (fcw_ref[0:1, c0:c1] * ext_fc[HALO - 2:HALO - 2 + tm, c0:c1]
                + fcw_ref[1:2, c0:c1] * ext_fc[HALO - 1:HALO - 1 + tm, c0:c1]
                + fcw_ref[2:3, c0:c1] * ext_fc[HALO:HALO + tm, c0:c1]) + fcb_ref[:, c0:c1]

    hh = (jax.nn.silu(conv(0, dff)) * conv(dff, 2 * dff)).astype(BF16)
    x2 = x1 + _dot(hh, wdn_ref[...])
    y_ref[...] = _rms(x2, gf_ref[...]) if final else x2

    @pl.when(t == last)
    def _():
        fcn_ref[...] = ext_fc[HALO + tm - 2:HALO + tm, :]

    ext_fc[HALO - 2:HALO, :] = ext_fc[HALO + tm - 2:HALO + tm, :]


def _prompt_post(l, x, att, g0, mp, w, tm, final):
    bsz, t, d = x.shape
    dff = w['w_down'].shape[1]
    row = lambda b, i: (b, i, 0)
    tile = pl.BlockSpec((None, tm, d), row)
    in_specs = [
        tile, tile, tile, tile,
        _const_spec((None, None, d, d), (l, 0, 0, 0)),
        _const_spec((None, d, d), (l, 0, 0)),
        _const_spec((None, 1, d), (l, 0, 0)),
        _const_spec((None, d, 2 * dff), (l, 0, 0)),
        _const_spec((None, 3, 2 * dff), (l, 0, 0)),
        _const_spec((None, 1, 2 * dff), (l, 0, 0)),
        _const_spec((None, dff, d), (l, 0, 0)),
        _const_spec((1, d), (0, 0)),
    ]
    return pl.pallas_call(
        functools.partial(_post_kernel, tm=tm, d=d, dff=dff, final=final),
        grid=(bsz, t // tm), in_specs=in_specs,
        out_specs=[tile, pl.BlockSpec((None, 2, 2 * dff), lambda b, i: (b, 0, 0))],
        out_shape=[jax.ShapeDtypeStruct((bsz, t, d), F32),
                   jax.ShapeDtypeStruct((bsz, 2, 2 * dff), F32)],
        scratch_shapes=[pltpu.VMEM((HALO + tm, 2 * dff), F32)],
        name=f"prompt_post_{l}",
        compiler_params=pltpu.CompilerParams(
            dimension_semantics=("arbitrary", "arbitrary"), vmem_limit_bytes=VMEM_LIMIT),
    )(x, att, g0, mp, w['w_branch'], w['w_o'], w['norm2'], w['w_up'], w['fconv_w'], w['fconv_b'],
      w['w_down'], w['norm_f'])


def _smix_kernel(x_ref, g1_ref, win_ref, scw_ref, rcw_ref, rcb_ref, wa_ref, ba_ref, wx_ref, bx_ref,
                 lam_ref, wb1_ref, wb2_ref, wm_ref, bm_ref, sst_ref, rst_ref, h0_ref,
                 q_ref, k_ref, v_ref, g0_ref, mp_ref, scn_ref, rcn_ref, hl_ref, *, d):
    xn = _rms(x_ref[...], g1_ref[...]).astype(BF16)
    q_ref[...] = _dot(xn, win_ref[:, 0:d])
    k_ref[...] = _dot(xn, win_ref[:, d:d + KV_W])
    v_ref[...] = _dot(xn, win_ref[:, d + KV_W:d + 2 * KV_W])
    c0 = d + 2 * KV_W

    u = _dot(xn, win_ref[:, c0 + d:c0 + 2 * d]) * _dot(xn, win_ref[:, c0 + 2 * d:c0 + 3 * d])
    sc = scw_ref[0:1, :] * sst_ref[0] + scw_ref[1:2, :] * sst_ref[1] + scw_ref[2:3, :] * u
    scn_ref[0] = sst_ref[1]
    scn_ref[1] = u
    o_sc = (_dot(xn, win_ref[:, c0:c0 + d]) * sc).astype(BF16)
    pb1 = _dot(o_sc, wb1_ref[...])

    x_lru = _dot(xn, win_ref[:, c0 + 3 * d:c0 + 4 * d])
    xc = (rcw_ref[0:1, :] * rst_ref[0] + rcw_ref[1:2, :] * rst_ref[1] + rcw_ref[2:3, :] * rst_ref[2]
          + rcw_ref[3:4, :] * x_lru) + rcb_ref[...]
    rcn_ref[0] = rst_ref[1]
    rcn_ref[1] = rst_ref[2]
    rcn_ref[2] = x_lru
    bw = d // LRU_BLOCKS
    xcb = xc.astype(BF16)
    r_lin = jnp.concatenate(
        [_dot(xcb[:, n * bw:(n + 1) * bw], wa_ref[n]) for n in range(LRU_BLOCKS)], axis=1)
    i_lin = jnp.concatenate(
        [_dot(xcb[:, n * bw:(n + 1) * bw], wx_ref[n]) for n in range(LRU_BLOCKS)], axis=1)
    a, b = _lru_coeffs(xc, r_lin + ba_ref[...], i_lin + bx_ref[...], lam_ref[...])
    h = a * h0_ref[...] + b
    hl_ref[...] = h
    o_lru = (h * _gelu_tanh(_dot(xn, win_ref[:, c0 + 4 * d:c0 + 5 * d]))).astype(BF16)
    pb2 = _dot(o_lru, wb2_ref[...])

    g0_ref[...] = jax.nn.sigmoid(_dot(xn, wm_ref[:, 0:d]) + bm_ref[:, 0:d])
    mp_ref[...] = (jax.nn.sigmoid(_dot(xn, wm_ref[:, d:2 * d]) + bm_ref[:, d:2 * d]) * pb1
                   + jax.nn.sigmoid(_dot(xn, wm_ref[:, 2 * d:3 * d]) + bm_ref[:, 2 * d:3 * d]) * pb2)


def _sample_mix(l, x, sst, rst, h0, w):
    n, d = x.shape
    n_in = w['w_in'].shape[-1]
    full = lambda shape: pl.BlockSpec(shape, lambda i: (0,) * len(shape))
    lay = lambda shape, idx: pl.BlockSpec(shape, lambda i: idx)
    vec = lambda m: lay((None, 1, m), (l, 0, 0))
    lw = (None, LRU_BLOCKS, d // LRU_BLOCKS, d // LRU_BLOCKS)
    in_specs = [
        full((n, d)), vec(d), lay((None, d, n_in), (l, 0, 0)), lay((None, 3, d), (l, 0, 0)),
        lay((None, 4, d), (l, 0, 0)), vec(d), lay(lw, (l, 0, 0, 0)), vec(d), lay(lw, (l, 0, 0, 0)),
        vec(d), vec(d), lay((None, None, d, d), (l, 1, 0, 0)), lay((None, None, d, d), (l, 2, 0, 0)),
        lay((None, d, 3 * d), (l, 0, 0)), vec(3 * d),
        lay((None, 2, n, d), (l, 0, 0, 0)), lay((None, 3, n, d), (l, 0, 0, 0)), lay((None, n, d), (l, 0, 0)),
    ]
    out_shape = [
        jax.ShapeDtypeStruct((n, d), F32), jax.ShapeDtypeStruct((n, KV_W), F32),
        jax.ShapeDtypeStruct((n, KV_W), F32), jax.ShapeDtypeStruct((n, d), F32),
        jax.ShapeDtypeStruct((n, d), F32), jax.ShapeDtypeStruct((2, n, d), F32),
        jax.ShapeDtypeStruct((3, n, d), F32), jax.ShapeDtypeStruct((n, d), F32),
    ]
    out_specs = [full(s.shape) for s in out_shape]
    return pl.pallas_call(
        functools.partial(_smix_kernel, d=d),
        grid=(1,), in_specs=in_specs, out_specs=out_specs, out_shape=out_shape,
        name=f"sample_mix_{l}",
        compiler_params=pltpu.CompilerParams(
            dimension_semantics=("arbitrary",), vmem_limit_bytes=VMEM_LIMIT),
    )(x, w['norm1'], w['w_in'], w['sconv_w'], w['rconv_w'], w['rconv_b'], w['lru_wa'], w['lru_ba'],
      w['lru_wx'], w['lru_bx'], w['lru_lambda'], w['w_branch'], w['w_branch'], w['w_merge'],
      w['b_merge'], sst, rst, h0)


def _spost_kernel(x_ref, att_ref, g0_ref, mp_ref, wb0_ref, wo_ref, g2_ref, wup_ref, fcw_ref, fcb_ref,
                  wdn_ref, gf_ref, fst_ref, y_ref, fcn_ref, *, dff, final):
    merged = g0_ref[...] * _dot(att_ref[...].astype(BF16), wb0_ref[...]) + mp_ref[...]
    x1 = x_ref[...] + _dot(merged.astype(BF16), wo_ref[...])
    xn2 = _rms(x1, g2_ref[...]).astype(BF16)
    up = _dot(xn2, wup_ref[...])
    c = (fcw_ref[0:1, :] * fst_ref[0] + fcw_ref[1:2, :] * fst_ref[1] + fcw_ref[2:3, :] * up) + fcb_ref[...]
    fcn_ref[0] = fst_ref[1]
    fcn_ref[1] = up
    hh = (jax.nn.silu(c[:, 0:dff]) * c[:, dff:2 * dff]).astype(BF16)
    x2 = x1 + _dot(hh, wdn_ref[...])
    y_ref[...] = _rms(x2, gf_ref[...]) if final else x2


def _sample_post(l, x, att, g0, mp, fst, w, final):
    n, d = x.shape
    dff = w['w_down'].shape[1]
    full = lambda shape: pl.BlockSpec(shape, lambda i: (0,) * len(shape))
    lay = lambda shape, idx: pl.BlockSpec(shape, lambda i: idx)
    in_specs = [
        full((n, d)), full((n, d)), full((n, d)), full((n, d)),
        lay((None, None, d, d), (l, 0, 0, 0)), lay((None, d, d), (l, 0, 0)), lay((None, 1, d), (l, 0, 0)),
        lay((None, d, 2 * dff), (l, 0, 0)), lay((None, 3, 2 * dff), (l, 0, 0)),
        lay((None, 1, 2 * dff), (l, 0, 0)), lay((None, dff, d), (l, 0, 0)), full((1, d)),
        lay((None, 2, n, 2 * dff), (l, 0, 0, 0)),
    ]
    out_shape = [jax.ShapeDtypeStruct((n, d), F32), jax.ShapeDtypeStruct((2, n, 2 * dff), F32)]
    return pl.pallas_call(
        functools.partial(_spost_kernel, dff=dff, final=final),
        grid=(1,), in_specs=in_specs, out_specs=[full(s.shape) for s in out_shape], out_shape=out_shape,
        name=f"sample_post_{l}",
        compiler_params=pltpu.CompilerParams(
            dimension_semantics=("arbitrary",), vmem_limit_bytes=VMEM_LIMIT),
    )(x, att, g0, mp, w['w_branch'], w['w_o'], w['norm2'], w['w_up'], w['fconv_w'], w['fconv_b'],
      w['w_down'], w['norm_f'], fst)


def _cache_mean_kernel(pt_ref, cache_ref, o_ref, buf, sem, *, n_b, n_chunks, n_pages):
    step = pl.program_id(0)
    n_steps = pl.num_programs(0)

    def copies(s, slot):
        l = s // (n_b * n_chunks)
        b = (s // n_chunks) % n_b
        c = s % n_chunks
        out = []
        for p in range(MEAN_PAGES):
            page = pt_ref[b * n_pages + c * MEAN_PAGES + p]
            out.append(pltpu.make_async_copy(cache_ref.at[l, page], buf.at[slot, p], sem.at[slot, p]))
        return out

    @pl.when(step == 0)
    def _():
        for cp in copies(step, 0):
            cp.start()

    slot = step % 2

    @pl.when(step + 1 < n_steps)
    def _():
        for cp in copies(step + 1, 1 - slot):
            cp.start()

    for cp in copies(step, slot):
        cp.wait()
    for j in range(MEAN_PAGES // PAGES_PER_BLOCK):
        tot = jnp.sum(buf[slot, PAGES_PER_BLOCK * j], axis=0, keepdims=True)
        for p in range(1, PAGES_PER_BLOCK):
            tot = tot + jnp.sum(buf[slot, PAGES_PER_BLOCK * j + p], axis=0, keepdims=True)
        o_ref[j:j + 1, :] = tot * (1.0 / MOBA_BLOCK)


def _cache_block_means(cache_k, page_table):
    depth, n_pool, page, hkv, hd = cache_k.shape
    n_b, n_pages = page_table.shape
    n_chunks = n_pages // MEAN_PAGES
    blocks_per_chunk = MEAN_PAGES // PAGES_PER_BLOCK
    cache2 = cache_k.reshape(depth, n_pool, page, hkv * hd)
    grid_spec = pltpu.PrefetchScalarGridSpec(
        num_scalar_prefetch=1,
        grid=(depth * n_b * n_chunks,),
        in_specs=[pl.BlockSpec(memory_space=pl.ANY)],
        out_specs=pl.BlockSpec(
            (None, None, blocks_per_chunk, hkv * hd),
            lambda s, pt: (s // (n_b * n_chunks), (s // n_chunks) % n_b, s % n_chunks, 0)),
        scratch_shapes=[pltpu.VMEM((2, MEAN_PAGES, page, hkv * hd), F32),
                        pltpu.SemaphoreType.DMA((2, MEAN_PAGES))],
    )
    return pl.pallas_call(
        functools.partial(_cache_mean_kernel, n_b=n_b, n_chunks=n_chunks, n_pages=n_pages),
        grid_spec=grid_spec,
        out_shape=jax.ShapeDtypeStruct((depth, n_b, n_pages // PAGES_PER_BLOCK, hkv * hd), F32),
        name="cache_block_means",
        compiler_params=pltpu.CompilerParams(
            dimension_semantics=("arbitrary",), vmem_limit_bytes=VMEM_LIMIT),
    )(page_table.reshape(-1), cache2)


def _choose_kernel(q_ref, km_ref, o_ref, *, nb):
    lane = lax.broadcasted_iota(jnp.int32, (nb, HEAD_DIM), 1)
    gate = jnp.zeros((nb, HEAD_DIM), F32)
    for h in range(N_HEADS):
        kv = h // GROUP
        gh = jnp.sum(km_ref[:, kv * HEAD_DIM:(kv + 1) * HEAD_DIM] * q_ref[h:h + 1, :], axis=1, keepdims=True)
        gate = jnp.where(lane == h, gh, gate)
    blk = lax.broadcasted_iota(jnp.int32, (nb, HEAD_DIM), 0)
    row = lax.broadcasted_iota(jnp.int32, (SUBLANES, HEAD_DIM), 0)
    out = jnp.zeros((SUBLANES, HEAD_DIM), jnp.int32)
    for r in range(MOBA_TOPK):
        best = jnp.max(gate, axis=0, keepdims=True)
        idx = jnp.min(jnp.where(gate == best, blk, nb), axis=0, keepdims=True)
        out = jnp.where(row == r, idx, out)
        gate = jnp.where(blk == idx, -jnp.inf, gate)
    o_ref[...] = out


def _sample_choose(q, kmean_l):
    n = q.shape[0]
    nb = kmean_l.shape[1]
    return pl.pallas_call(
        functools.partial(_choose_kernel, nb=nb),
        grid=(n,),
        in_specs=[pl.BlockSpec((None, N_HEADS, HEAD_DIM), lambda b: (b, 0, 0)),
                  pl.BlockSpec((None, nb, KV_W), lambda b: (b, 0, 0))],
        out_specs=pl.BlockSpec((None, SUBLANES, HEAD_DIM), lambda b: (b, 0, 0)),
        out_shape=jax.ShapeDtypeStruct((n, SUBLANES, HEAD_DIM), jnp.int32),
        name="sample_choose",
        compiler_params=pltpu.CompilerParams(dimension_semantics=("arbitrary",)),
    )(q.reshape(n, N_HEADS, HEAD_DIM), kmean_l)


def _sattn_kernel(sel_ref, pt_ref, q_ref, kn_ref, vn_ref, ck_ref, cv_ref, o_ref, kbuf, vbuf, sem,
                  *, l, n_pages):
    b = pl.program_id(0)
    n_sel_pages = MOBA_TOPK * PAGES_PER_BLOCK

    def copies(h):
        kv = h // GROUP
        out = []
        for r in range(MOBA_TOPK):
            blk = sel_ref[(b * MOBA_TOPK + r) * N_HEADS + h]
            for p in range(PAGES_PER_BLOCK):
                page = pt_ref[b * n_pages + blk * PAGES_PER_BLOCK + p]
                slot = r * PAGES_PER_BLOCK + p
                for c, (src, dst) in enumerate(((ck_ref, kbuf), (cv_ref, vbuf))):
                    out.append(pltpu.make_async_copy(
                        src.at[l, page, :, pl.ds(kv * HEAD_DIM, HEAD_DIM)],
                        dst.at[h, pl.ds(slot * PAGE_SIZE, PAGE_SIZE), :],
                        sem.at[c, h, slot]))
        return out

    for h in range(N_HEADS):
        for cp in copies(h):
            cp.start()
    for h in range(N_HEADS):
        kv = h // GROUP
        for cp in copies(h):
            cp.wait()
        q = q_ref[h:h + 1, :]
        s = jnp.sum(kbuf[h] * q, axis=1, keepdims=True) * ATT_SCALE
        s_new = jnp.sum(kn_ref[kv:kv + 1, :] * q, axis=1, keepdims=True) * ATT_SCALE
        m = jnp.maximum(jnp.max(s, axis=0, keepdims=True), s_new)
        p = jnp.exp(s - m)
        p_new = jnp.exp(s_new - m)
        den = jnp.sum(p, axis=0, keepdims=True) + p_new
        num = jnp.sum(p * vbuf[h], axis=0, keepdims=True) + p_new * vn_ref[kv:kv + 1, :]
        o_ref[h:h + 1, :] = num / den
    del n_sel_pages


def _sample_attn(l, sel, page_table, q, k_new, v_new, cache_k, cache_v):
    n = q.shape[0]
    depth, n_pool, page, hkv, hd = cache_k.shape
    n_pages = page_table.shape[1]
    rows = MOBA_TOPK * MOBA_BLOCK
    grid_spec = pltpu.PrefetchScalarGridSpec(
        num_scalar_prefetch=2,
        grid=(n,),
        in_specs=[pl.BlockSpec((None, N_HEADS, HEAD_DIM), lambda b, *_: (b, 0, 0)),
                  pl.BlockSpec((None, N_KV_HEADS, HEAD_DIM), lambda b, *_: (b, 0, 0)),
                  pl.BlockSpec((None, N_KV_HEADS, HEAD_DIM), lambda b, *_: (b, 0, 0)),
                  pl.BlockSpec(memory_space=pl.ANY),
                  pl.BlockSpec(memory_space=pl.ANY)],
        out_specs=pl.BlockSpec((None, N_HEADS, HEAD_DIM), lambda b, *_: (b, 0, 0)),
        scratch_shapes=[pltpu.VMEM((N_HEADS, rows, HEAD_DIM), F32),
                        pltpu.VMEM((N_HEADS, rows, HEAD_DIM), F32),
                        pltpu.SemaphoreType.DMA((2, N_HEADS, MOBA_TOPK * PAGES_PER_BLOCK))],
    )
    sel_flat = sel[:, :MOBA_TOPK, :N_HEADS].reshape(-1)
    out = pl.pallas_call(
        functools.partial(_sattn_kernel, l=l, n_pages=n_pages),
        grid_spec=grid_spec,
        out_shape=jax.ShapeDtypeStruct((n, N_HEADS, HEAD_DIM), F32),
        name=f"sample_attn_{l}",
        compiler_params=pltpu.CompilerParams(dimension_semantics=("arbitrary",)),
    )(sel_flat, page_table.reshape(-1), q.reshape(n, N_HEADS, HEAD_DIM),
      k_new.reshape(n, N_KV_HEADS, HEAD_DIM), v_new.reshape(n, N_KV_HEADS, HEAD_DIM),
      cache_k.reshape(depth, n_pool, page, hkv * hd), cache_v.reshape(depth, n_pool, page, hkv * hd))
    return out.reshape(n, N_HEADS * HEAD_DIM)


def _trunk(x_prompt, x_sample, cache_k, cache_v, state_sconv, state_rconv, state_lru, state_fconv,
           page_table, w, tm):
    depth = w['w_in'].shape[0]
    bsz, t, d = x_prompt.shape
    n = x_sample.shape[0]
    xp = x_prompt
    xs = x_sample.reshape(n, d)
    sst = jnp.swapaxes(state_sconv, 1, 2)
    rst = jnp.swapaxes(state_rconv, 1, 2)
    fst = jnp.swapaxes(state_fconv, 1, 2)
    kmean_s = _cache_block_means(cache_k, page_table)

    st_p, st_s = [], []
    for l in range(depth):
        final = l == depth - 1
        q, k, v, km, g0, mp, scn, rcn, hl = _prompt_mix(l, xp, w, tm)
        att = _prompt_attn(q, k, v, km.reshape(bsz, t // MOBA_BLOCK, KV_W))
        xp, fcn = _prompt_post(l, xp, att, g0, mp, w, tm, final)
        st_p.append((k.reshape(bsz, t, N_KV_HEADS, HEAD_DIM), v.reshape(bsz, t, N_KV_HEADS, HEAD_DIM),
                     scn, rcn, hl.reshape(bsz, d), fcn))

        qs, ks, vs, g0s, mps, scns, rcns, hls = _sample_mix(l, xs, sst, rst, state_lru, w)
        sel = _sample_choose(qs, kmean_s[l])
        atts = _sample_attn(l, sel, page_table, qs, ks, vs, cache_k, cache_v)
        xs, fcns = _sample_post(l, xs, atts, g0s, mps, fst, w, final)
        st_s.append((ks.reshape(n, 1, N_KV_HEADS, HEAD_DIM), vs.reshape(n, 1, N_KV_HEADS, HEAD_DIM),
                     jnp.swapaxes(scns, 0, 1), jnp.swapaxes(rcns, 0, 1), hls, jnp.swapaxes(fcns, 0, 1)))

    outs_p = [jnp.stack([s[i] for s in st_p], axis=0) for i in range(6)]
    outs_s = [jnp.stack([s[i] for s in st_s], axis=0) for i in range(6)]
    return (xp, xs.reshape(n, 1, d), *outs_p, *outs_s)


def kernel(x_prompt, x_sample, cache_k, cache_v, state_sconv, state_rconv, state_lru, state_fconv, page_table, norm1, w_in, sconv_w, rconv_w, rconv_b, lru_wa, lru_ba, lru_wx, lru_bx, lru_lambda, w_branch, w_merge, b_merge, w_o, norm2, w_up, fconv_w, fconv_b, w_down, norm_f):
    row = lambda a: a.reshape(a.shape[0], 1, a.shape[1])
    w = dict(
        norm1=row(norm1), w_in=w_in.astype(BF16), sconv_w=sconv_w, rconv_w=rconv_w, rconv_b=row(rconv_b),
        lru_wa=lru_wa.astype(BF16), lru_ba=row(lru_ba), lru_wx=lru_wx.astype(BF16), lru_bx=row(lru_bx),
        lru_lambda=row(lru_lambda), w_branch=w_branch.astype(BF16), w_merge=w_merge.astype(BF16),
        b_merge=row(b_merge), w_o=w_o.astype(BF16), norm2=row(norm2), w_up=w_up.astype(BF16),
        fconv_w=fconv_w, fconv_b=row(fconv_b), w_down=w_down.astype(BF16), norm_f=norm_f.reshape(1, -1),
    )
    return _trunk(x_prompt, x_sample, cache_k, cache_v, state_sconv, state_rconv, state_lru, state_fconv,
                  page_table, w, TIME_TILE)
```

```python
import functools
import math

import jax
import jax.numpy as jnp
from jax import lax
from jax.experimental import pallas as pl
from jax.experimental.pallas import tpu as pltpu

F32 = jnp.float32
BF16 = jnp.bfloat16

HEAD_DIM = 128
N_HEADS = 8
N_KV_HEADS = 4
GROUP = N_HEADS // N_KV_HEADS
KV_W = N_KV_HEADS * HEAD_DIM
MOBA_BLOCK = 256
MOBA_TOPK = 3
LRU_BLOCKS = 8
LRU_C = 8.0
RMS_EPS = 1e-6
PAGE_SIZE = 128
PAGES_PER_BLOCK = MOBA_BLOCK // PAGE_SIZE
ATT_SCALE = HEAD_DIM ** -0.5

SUBLANES = 8
HALO = SUBLANES
VMEM_LIMIT = 56 * 1024 * 1024
TIME_TILE = 256
MEAN_PAGES = 16


def _dot(a, b):
    return jnp.dot(a, b, preferred_element_type=F32)


def _dot_nt(a, b, precision=None):
    return lax.dot_general(a, b, (((1,), (1,)), ((), ())), preferred_element_type=F32,
                           precision=precision)


def _dot_tn(a, b):
    return lax.dot_general(a, b, (((0,), (0,)), ((), ())), preferred_element_type=F32)


def _rms(x, g):
    return x * lax.rsqrt(jnp.mean(x * x, axis=-1, keepdims=True) + RMS_EPS) * g


def _gelu_tanh(x):
    c = math.sqrt(2.0 / math.pi)
    return x * (0.5 * (1.0 + jnp.tanh(c * (x + 0.044715 * (x * x * x)))))


def _softplus(x):
    return jnp.maximum(x, 0.0) + jnp.log1p(jnp.exp(-jnp.abs(x)))


def _lru_coeffs(xc, r_lin, i_lin, lam):
    r = jax.nn.sigmoid(r_lin)
    i = jax.nn.sigmoid(i_lin)
    log_a = -LRU_C * r * _softplus(-lam)
    a = jnp.exp(log_a)
    th = jnp.tanh(log_a)
    b = jnp.sqrt(-2.0 * th / (1.0 - th)) * (i * xc)
    return a, b


def _const_spec(shape, index):
    return pl.BlockSpec(shape, lambda *_: index, pipeline_mode=pl.Buffered(1))


def _mix_kernel(x_ref, g1_ref, win_ref, scw_ref, rcw_ref, rcb_ref, wa_ref, ba_ref, wx_ref, bx_ref,
                lam_ref, wb1_ref, wb2_ref, wm_ref, bm_ref,
                q_ref, kb_ref, vb_ref, kf_ref, vf_ref, km_ref, g0_ref, mp_ref, scn_ref, rcn_ref, hl_ref,
                ext_sc, ext_rc, lin_a, lin_i, h_carry, *, tm, d):
    t = pl.program_id(1)
    last = pl.num_programs(1) - 1

    @pl.when(t == 0)
    def _():
        ext_sc[0:HALO, :] = jnp.zeros((HALO, d), F32)
        ext_rc[0:HALO, :] = jnp.zeros((HALO, d), F32)
        h_carry[...] = jnp.zeros((1, d), F32)

    xn = _rms(x_ref[...], g1_ref[...]).astype(BF16)

    q_ref[...] = _dot(xn, win_ref[:, 0:d]).astype(BF16)
    k = _dot(xn, win_ref[:, d:d + KV_W])
    v = _dot(xn, win_ref[:, d + KV_W:d + 2 * KV_W])
    kb_ref[...] = k.astype(BF16)
    vb_ref[...] = v.astype(BF16)
    for h in range(N_KV_HEADS):
        kf_ref[pl.ds(h, tm, stride=N_KV_HEADS), :] = k[:, h * HEAD_DIM:(h + 1) * HEAD_DIM]
        vf_ref[pl.ds(h, tm, stride=N_KV_HEADS), :] = v[:, h * HEAD_DIM:(h + 1) * HEAD_DIM]
    for j in range(tm // MOBA_BLOCK):
        km_ref[j:j + 1, :] = jnp.mean(k[j * MOBA_BLOCK:(j + 1) * MOBA_BLOCK], axis=0, keepdims=True)
    c0 = d + 2 * KV_W

    u = _dot(xn, win_ref[:, c0 + d:c0 + 2 * d]) * _dot(xn, win_ref[:, c0 + 2 * d:c0 + 3 * d])
    ext_sc[HALO:HALO + tm, :] = u
    sc = (scw_ref[0:1, :] * ext_sc[HALO - 2:HALO - 2 + tm, :]
          + scw_ref[1:2, :] * ext_sc[HALO - 1:HALO - 1 + tm, :]
          + scw_ref[2:3, :] * u)
    o_sc = (_dot(xn, win_ref[:, c0:c0 + d]) * sc).astype(BF16)
    pb1 = _dot(o_sc, wb1_ref[...])

    ext_rc[HALO:HALO + tm, :] = _dot(xn, win_ref[:, c0 + 3 * d:c0 + 4 * d])
    xc = (rcw_ref[0:1, :] * ext_rc[HALO - 3:HALO - 3 + tm, :]
          + rcw_ref[1:2, :] * ext_rc[HALO - 2:HALO - 2 + tm, :]
          + rcw_ref[2:3, :] * ext_rc[HALO - 1:HALO - 1 + tm, :]
          + rcw_ref[3:4, :] * ext_rc[HALO:HALO + tm, :]) + rcb_ref[...]
    bw = d // LRU_BLOCKS
    xcb = xc.astype(BF16)
    for n in range(LRU_BLOCKS):
        lin_a[:, n * bw:(n + 1) * bw] = _dot(xcb[:, n * bw:(n + 1) * bw], wa_ref[n])
        lin_i[:, n * bw:(n + 1) * bw] = _dot(xcb[:, n * bw:(n + 1) * bw], wx_ref[n])
    a, b = _lru_coeffs(xc, lin_a[...] + ba_ref[...], lin_i[...] + bx_ref[...], lam_ref[...])
    lin_a[...] = a
    lin_i[...] = b
    row = lax.broadcasted_iota(jnp.int32, (SUBLANES, d), 0)
    h_prev = jnp.broadcast_to(h_carry[...], (SUBLANES, d))
    for g in range(tm // SUBLANES):
        rows = slice(g * SUBLANES, (g + 1) * SUBLANES)
        ag = lin_a[rows, :]
        bg = lin_i[rows, :]
        s = 1
        while s < SUBLANES:
            keep = row >= s
            bg = bg + ag * jnp.where(keep, pltpu.roll(bg, s, 0), 0.0)
            ag = ag * jnp.where(keep, pltpu.roll(ag, s, 0), 1.0)
            s *= 2
        hg = ag * h_prev + bg
        lin_i[rows, :] = hg
        h_prev = jnp.broadcast_to(hg[SUBLANES - 1:SUBLANES, :], (SUBLANES, d))
    h_carry[...] = h_prev[0:1, :]
    h = lin_i[...]
    o_lru = (h * _gelu_tanh(_dot(xn, win_ref[:, c0 + 4 * d:c0 + 5 * d]))).astype(BF16)
    pb2 = _dot(o_lru, wb2_ref[...])

    g0_ref[...] = jax.nn.sigmoid(_dot(xn, wm_ref[:, 0:d]) + bm_ref[:, 0:d])
    mp_ref[...] = (jax.nn.sigmoid(_dot(xn, wm_ref[:, d:2 * d]) + bm_ref[:, d:2 * d]) * pb1
                   + jax.nn.sigmoid(_dot(xn, wm_ref[:, 2 * d:3 * d]) + bm_ref[:, 2 * d:3 * d]) * pb2)

    @pl.when(t == last)
    def _():
        scn_ref[...] = ext_sc[HALO + tm - 2:HALO + tm, :]
        rcn_ref[...] = ext_rc[HALO + tm - 3:HALO + tm, :]
        hl_ref[...] = h_carry[...]

    ext_sc[HALO - 2:HALO, :] = ext_sc[HALO + tm - 2:HALO + tm, :]
    ext_rc[HALO - 3:HALO, :] = ext_rc[HALO + tm - 3:HALO + tm, :]


def _prompt_mix(l, x, w, tm):
    bsz, t, d = x.shape
    nt = t // tm
    nmb = tm // MOBA_BLOCK
    n_in = w['w_in'].shape[-1]
    row = lambda b, i: (b, i, 0)
    per_b = lambda b, i: (b, 0, 0)
    vec = lambda n: _const_spec((None, 1, n), (l, 0, 0))
    in_specs = [
        pl.BlockSpec((None, tm, d), row),
        vec(d),
        _const_spec((None, d, n_in), (l, 0, 0)),
        _const_spec((None, 3, d), (l, 0, 0)),
        _const_spec((None, 4, d), (l, 0, 0)),
        vec(d),
        _const_spec((None, LRU_BLOCKS, d // LRU_BLOCKS, d // LRU_BLOCKS), (l, 0, 0, 0)),
        vec(d),
        _const_spec((None, LRU_BLOCKS, d // LRU_BLOCKS, d // LRU_BLOCKS), (l, 0, 0, 0)),
        vec(d),
        vec(d),
        _const_spec((None, None, d, d), (l, 1, 0, 0)),
        _const_spec((None, None, d, d), (l, 2, 0, 0)),
        _const_spec((None, d, 3 * d), (l, 0, 0)),
        vec(3 * d),
    ]
    out_shape = [
        jax.ShapeDtypeStruct((bsz, t, d), BF16),
        jax.ShapeDtypeStruct((bsz, t, KV_W), BF16),
        jax.ShapeDtypeStruct((bsz, t, KV_W), BF16),
        jax.ShapeDtypeStruct((bsz, t * N_KV_HEADS, HEAD_DIM), F32),
        jax.ShapeDtypeStruct((bsz, t * N_KV_HEADS, HEAD_DIM), F32),
        jax.ShapeDtypeStruct((bsz, nt, nmb, KV_W), F32),
        jax.ShapeDtypeStruct((bsz, t, d), F32),
        jax.ShapeDtypeStruct((bsz, t, d), F32),
        jax.ShapeDtypeStruct((bsz, 2, d), F32),
        jax.ShapeDtypeStruct((bsz, 3, d), F32),
        jax.ShapeDtypeStruct((bsz, 1, d), F32),
    ]
    out_specs = [
        pl.BlockSpec((None, tm, d), row),
        pl.BlockSpec((None, tm, KV_W), row),
        pl.BlockSpec((None, tm, KV_W), row),
        pl.BlockSpec((None, tm * N_KV_HEADS, HEAD_DIM), row),
        pl.BlockSpec((None, tm * N_KV_HEADS, HEAD_DIM), row),
        pl.BlockSpec((None, None, nmb, KV_W), lambda b, i: (b, i, 0, 0)),
        pl.BlockSpec((None, tm, d), row),
        pl.BlockSpec((None, tm, d), row),
        pl.BlockSpec((None, 2, d), per_b),
        pl.BlockSpec((None, 3, d), per_b),
        pl.BlockSpec((None, 1, d), per_b),
    ]
    scratch = [
        pltpu.VMEM((HALO + tm, d), F32),
        pltpu.VMEM((HALO + tm, d), F32),
        pltpu.VMEM((tm, d), F32),
        pltpu.VMEM((tm, d), F32),
        pltpu.VMEM((1, d), F32),
    ]
    return pl.pallas_call(
        functools.partial(_mix_kernel, tm=tm, d=d),
        grid=(bsz, nt), in_specs=in_specs, out_specs=out_specs, out_shape=out_shape,
        scratch_shapes=scratch, name=f"prompt_mix_{l}",
        compiler_params=pltpu.CompilerParams(
            dimension_semantics=("arbitrary", "arbitrary"), vmem_limit_bytes=VMEM_LIMIT),
    )(x, w['norm1'], w['w_in'], w['sconv_w'], w['rconv_w'], w['rconv_b'], w['lru_wa'], w['lru_ba'],
      w['lru_wx'], w['lru_bx'], w['lru_lambda'], w['w_branch'], w['w_branch'], w['w_merge'],
      w['b_merge'])


def _attn_kernel(q_ref, k_ref, v_ref, km_ref, o_ref, *, nb):
    blk = MOBA_BLOCK
    nq = GROUP * blk

    parts = []
    rest = km_ref[...]
    for _ in range(3):
        part = rest.astype(BF16).astype(F32)
        parts += [part, jnp.zeros_like(part)]
        rest = rest - part
    km_terms = jnp.concatenate(parts, axis=0).astype(BF16)
    prow = 2 * nb

    block_id = lax.broadcasted_iota(jnp.int32, (nb, nq), 0)
    key = lax.broadcasted_iota(jnp.int32, (blk, nq), 0)
    qpos = lax.broadcasted_iota(jnp.int32, (blk, nq), 1) & (blk - 1)
    causal = key <= qpos

    for i in range(nb):
        q_rows = slice(i * blk, (i + 1) * blk)
        n_keys = (i + 1) * blk
        q2 = jnp.concatenate([q_ref[q_rows, g * HEAD_DIM:(g + 1) * HEAD_DIM] for g in range(GROUP)], axis=0)
        if i == 0:
            s_all = _dot_nt(k_ref[0:n_keys, :], q2)
        else:
            s_all = _dot_nt(jnp.concatenate([k_ref[0:n_keys, :], km_terms], axis=0), q2)
            gate = (s_all[n_keys + 2 * prow:n_keys + 2 * prow + nb]
                    + s_all[n_keys + prow:n_keys + prow + nb] + s_all[n_keys:n_keys + nb])
            past = block_id < i
            gate = jnp.where(past, gate, -jnp.inf)
            rank = jnp.zeros((nb, nq), jnp.int32)
            for j in range(i):
                gj = gate[j:j + 1, :]
                rank = rank + jnp.where((gj > gate) | ((gj == gate) & (j < block_id)), 1, 0)
            sel = jnp.where(past & (rank < MOBA_TOPK), 1.0, 0.0)

        pieces = [jnp.where(sel[j:j + 1, :] > 0.5, s_all[j * blk:(j + 1) * blk] * ATT_SCALE, -jnp.inf)
                  for j in range(i)]
        pieces.append(jnp.where(causal, s_all[i * blk:n_keys] * ATT_SCALE, -jnp.inf))
        m = jnp.max(pieces[-1], axis=0, keepdims=True)
        for piece in pieces[:-1]:
            m = jnp.maximum(m, jnp.max(piece, axis=0, keepdims=True))
        probs = [jnp.exp(piece - m) for piece in pieces]
        den = jnp.sum(probs[0], axis=0, keepdims=True)
        for pr in probs[1:]:
            den = den + jnp.sum(pr, axis=0, keepdims=True)
        p_all = jnp.concatenate([pr.astype(BF16) for pr in probs], axis=0)
        out = (_dot_tn(v_ref[0:n_keys, :], p_all) / den).T
        for g in range(GROUP):
            o_ref[q_rows, g * HEAD_DIM:(g + 1) * HEAD_DIM] = out[g * blk:(g + 1) * blk].astype(o_ref.dtype)


def _prompt_attn(q, k, v, kmean):
    bsz, t, d = q.shape
    nb = t // MOBA_BLOCK
    gw = GROUP * HEAD_DIM
    return pl.pallas_call(
        functools.partial(_attn_kernel, nb=nb),
        grid=(bsz, N_KV_HEADS),
        in_specs=[
            pl.BlockSpec((None, t, gw), lambda b, h: (b, 0, h)),
            pl.BlockSpec((None, t, HEAD_DIM), lambda b, h: (b, 0, h)),
            pl.BlockSpec((None, t, HEAD_DIM), lambda b, h: (b, 0, h)),
            pl.BlockSpec((None, nb, HEAD_DIM), lambda b, h: (b, 0, h)),
        ],
        out_specs=pl.BlockSpec((None, t, gw), lambda b, h: (b, 0, h)),
        out_shape=jax.ShapeDtypeStruct((bsz, t, d), BF16),
        name="prompt_attn",
        compiler_params=pltpu.CompilerParams(
            dimension_semantics=("arbitrary", "arbitrary"), vmem_limit_bytes=VMEM_LIMIT),
    )(q, k, v, kmean)


def _post_kernel(x_ref, att_ref, g0_ref, mp_ref, wb0_ref, wo_ref, g2_ref, wup_ref, fcw_ref, fcb_ref,
                 wdn_ref, gf_ref, y_ref, fcn_ref, ext_fc, *, tm, d, dff, final):
    t = pl.program_id(1)
    last = pl.num_programs(1) - 1

    @pl.when(t == 0)
    def _():
        ext_fc[0:HALO, :] = jnp.zeros((HALO, 2 * dff), F32)

    merged = g0_ref[...] * _dot(att_ref[...], wb0_ref[...]) + mp_ref[...]
    x1 = x_ref[...] + _dot(merged.astype(BF16), wo_ref[...])
    xn2 = _rms(x1, g2_ref[...]).astype(BF16)
    ext_fc[HALO:HALO + tm, :] = _dot(xn2, wup_ref[...])

    def conv(c0, c1):
        return (fcw_ref[0:1, c0:c1] * ext_fc[HALO - 2:HALO - 2 + tm, c0:c1]
                + fcw_ref[1:2, c0:c1] * ext_fc[HALO - 1:HALO - 1 + tm, c0:c1]
                + fcw_ref[2:3, c0:c1] * ext_fc[HALO:HALO + tm, c0:c1]) + fcb_ref[:, c0:c1]

    hh = (jax.nn.silu(conv(0, dff)) * conv(dff, 2 * dff)).astype(BF16)
    x2 = x1 + _dot(hh, wdn_ref[...])
    y_ref[...] = _rms(x2, gf_ref[...]) if final else x2

    @pl.when(t == last)
    def _():
        fcn_ref[...] = ext_fc[HALO + tm - 2:HALO + tm, :]

    ext_fc[HALO - 2:HALO, :] = ext_fc[HALO + tm - 2:HALO + tm, :]


def _prompt_post(l, x, att, g0, mp, w, tm, final):
    bsz, t, d = x.shape
    dff = w['w_down'].shape[1]
    row = lambda b, i: (b, i, 0)
    tile = pl.BlockSpec((None, tm, d), row)
    in_specs = [
        tile, tile, tile, tile,
        _const_spec((None, None, d, d), (l, 0, 0, 0)),
        _const_spec((None, d, d), (l, 0, 0)),
        _const_spec((None, 1, d), (l, 0, 0)),
        _const_spec((None, d, 2 * dff), (l, 0, 0)),
        _const_spec((None, 3, 2 * dff), (l, 0, 0)),
        _const_spec((None, 1, 2 * dff), (l, 0, 0)),
        _const_spec((None, dff, d), (l, 0, 0)),
        _const_spec((1, d), (0, 0)),
    ]
    return pl.pallas_call(
        functools.partial(_post_kernel, tm=tm, d=d, dff=dff, final=final),
        grid=(bsz, t // tm), in_specs=in_specs,
        out_specs=[tile, pl.BlockSpec((None, 2, 2 * dff), lambda b, i: (b, 0, 0))],
        out_shape=[jax.ShapeDtypeStruct((bsz, t, d), F32),
                   jax.ShapeDtypeStruct((bsz, 2, 2 * dff), F32)],
        scratch_shapes=[pltpu.VMEM((HALO + tm, 2 * dff), F32)],
        name=f"prompt_post_{l}",
        compiler_params=pltpu.CompilerParams(
            dimension_semantics=("arbitrary", "arbitrary"), vmem_limit_bytes=VMEM_LIMIT),
    )(x, att, g0, mp, w['w_branch'], w['w_o'], w['norm2'], w['w_up'], w['fconv_w'], w['fconv_b'],
      w['w_down'], w['norm_f'])


def _smix_kernel(x_ref, g1_ref, win_ref, scw_ref, rcw_ref, rcb_ref, wa_ref, ba_ref, wx_ref, bx_ref,
                 lam_ref, wb1_ref, wb2_ref, wm_ref, bm_ref, sst_ref, rst_ref, h0_ref,
                 q_ref, k_ref, v_ref, g0_ref, mp_ref, scn_ref, rcn_ref, hl_ref, *, d):
    xn = _rms(x_ref[...], g1_ref[...]).astype(BF16)
    q_ref[...] = _dot(xn, win_ref[:, 0:d])
    k_ref[...] = _dot(xn, win_ref[:, d:d + KV_W])
    v_ref[...] = _dot(xn, win_ref[:, d + KV_W:d + 2 * KV_W])
    c0 = d + 2 * KV_W

    u = _dot(xn, win_ref[:, c0 + d:c0 + 2 * d]) * _dot(xn, win_ref[:, c0 + 2 * d:c0 + 3 * d])
    sc = scw_ref[0:1, :] * sst_ref[0] + scw_ref[1:2, :] * sst_ref[1] + scw_ref[2:3, :] * u
    scn_ref[0] = sst_ref[1]
    scn_ref[1] = u
    o_sc = (_dot(xn, win_ref[:, c0:c0 + d]) * sc).astype(BF16)
    pb1 = _dot(o_sc, wb1_ref[...])

    x_lru = _dot(xn, win_ref[:, c0 + 3 * d:c0 + 4 * d])
    xc = (rcw_ref[0:1, :] * rst_ref[0] + rcw_ref[1:2, :] * rst_ref[1] + rcw_ref[2:3, :] * rst_ref[2]
          + rcw_ref[3:4, :] * x_lru) + rcb_ref[...]
    rcn_ref[0] = rst_ref[1]
    rcn_ref[1] = rst_ref[2]
    rcn_ref[2] = x_lru
    bw = d // LRU_BLOCKS
    xcb = xc.astype(BF16)
    r_lin = jnp.concatenate(
        [_dot(xcb[:, n * bw:(n + 1) * bw], wa_ref[n]) for n in range(LRU_BLOCKS)], axis=1)
    i_lin = jnp.concatenate(
        [_dot(xcb[:, n * bw:(n + 1) * bw], wx_ref[n]) for n in range(LRU_BLOCKS)], axis=1)
    a, b = _lru_coeffs(xc, r_lin + ba_ref[...], i_lin + bx_ref[...], lam_ref[...])
    h = a * h0_ref[...] + b
    hl_ref[...] = h
    o_lru = (h * _gelu_tanh(_dot(xn, win_ref[:, c0 + 4 * d:c0 + 5 * d]))).astype(BF16)
    pb2 = _dot(o_lru, wb2_ref[...])

    g0_ref[...] = jax.nn.sigmoid(_dot(xn, wm_ref[:, 0:d]) + bm_ref[:, 0:d])
    mp_ref[...] = (jax.nn.sigmoid(_dot(xn, wm_ref[:, d:2 * d]) + bm_ref[:, d:2 * d]) * pb1
                   + jax.nn.sigmoid(_dot(xn, wm_ref[:, 2 * d:3 * d]) + bm_ref[:, 2 * d:3 * d]) * pb2)


def _sample_mix(l, x, sst, rst, h0, w):
    n, d = x.shape
    n_in = w['w_in'].shape[-1]
    full = lambda shape: pl.BlockSpec(shape, lambda i: (0,) * len(shape))
    lay = lambda shape, idx: pl.BlockSpec(shape, lambda i: idx)
    vec = lambda m: lay((None, 1, m), (l, 0, 0))
    lw = (None, LRU_BLOCKS, d // LRU_BLOCKS, d // LRU_BLOCKS)
    in_specs = [
        full((n, d)), vec(d), lay((None, d, n_in), (l, 0, 0)), lay((None, 3, d), (l, 0, 0)),
        lay((None, 4, d), (l, 0, 0)), vec(d), lay(lw, (l, 0, 0, 0)), vec(d), lay(lw, (l, 0, 0, 0)),
        vec(d), vec(d), lay((None, None, d, d), (l, 1, 0, 0)), lay((None, None, d, d), (l, 2, 0, 0)),
        lay((None, d, 3 * d), (l, 0, 0)), vec(3 * d),
        lay((None, 2, n, d), (l, 0, 0, 0)), lay((None, 3, n, d), (l, 0, 0, 0)), lay((None, n, d), (l, 0, 0)),
    ]
    out_shape = [
        jax.ShapeDtypeStruct((n, d), F32), jax.ShapeDtypeStruct((n, KV_W), F32),
        jax.ShapeDtypeStruct((n, KV_W), F32), jax.ShapeDtypeStruct((n, d), F32),
        jax.ShapeDtypeStruct((n, d), F32), jax.ShapeDtypeStruct((2, n, d), F32),
        jax.ShapeDtypeStruct((3, n, d), F32), jax.ShapeDtypeStruct((n, d), F32),
    ]
    out_specs = [full(s.shape) for s in out_shape]
    return pl.pallas_call(
        functools.partial(_smix_kernel, d=d),
        grid=(1,), in_specs=in_specs, out_specs=out_specs, out_shape=out_shape,
        name=f"sample_mix_{l}",
        compiler_params=pltpu.CompilerParams(
            dimension_semantics=("arbitrary",), vmem_limit_bytes=VMEM_LIMIT),
    )(x, w['norm1'], w['w_in'], w['sconv_w'], w['rconv_w'], w['rconv_b'], w['lru_wa'], w['lru_ba'],
      w['lru_wx'], w['lru_bx'], w['lru_lambda'], w['w_branch'], w['w_branch'], w['w_merge'],
      w['b_merge'], sst, rst, h0)


def _spost_kernel(x_ref, att_ref, g0_ref, mp_ref, wb0_ref, wo_ref, g2_ref, wup_ref, fcw_ref, fcb_ref,
                  wdn_ref, gf_ref, fst_ref, y_ref, fcn_ref, *, dff, final):
    merged = g0_ref[...] * _dot(att_ref[...].astype(BF16), wb0_ref[...]) + mp_ref[...]
    x1 = x_ref[...] + _dot(merged.astype(BF16), wo_ref[...])
    xn2 = _rms(x1, g2_ref[...]).astype(BF16)
    up = _dot(xn2, wup_ref[...])
    c = (fcw_ref[0:1, :] * fst_ref[0] + fcw_ref[1:2, :] * fst_ref[1] + fcw_ref[2:3, :] * up) + fcb_ref[...]
    fcn_ref[0] = fst_ref[1]
    fcn_ref[1] = up
    hh = (jax.nn.silu(c[:, 0:dff]) * c[:, dff:2 * dff]).astype(BF16)
    x2 = x1 + _dot(hh, wdn_ref[...])
    y_ref[...] = _rms(x2, gf_ref[...]) if final else x2


def _sample_post(l, x, att, g0, mp, fst, w, final):
    n, d = x.shape
    dff = w['w_down'].shape[1]
    full = lambda shape: pl.BlockSpec(shape, lambda i: (0,) * len(shape))
    lay = lambda shape, idx: pl.BlockSpec(shape, lambda i: idx)
    in_specs = [
        full((n, d)), full((n, d)), full((n, d)), full((n, d)),
        lay((None, None, d, d), (l, 0, 0, 0)), lay((None, d, d), (l, 0, 0)), lay((None, 1, d), (l, 0, 0)),
        lay((None, d, 2 * dff), (l, 0, 0)), lay((None, 3, 2 * dff), (l, 0, 0)),
        lay((None, 1, 2 * dff), (l, 0, 0)), lay((None, dff, d), (l, 0, 0)), full((1, d)),
        lay((None, 2, n, 2 * dff), (l, 0, 0, 0)),
    ]
    out_shape = [jax.ShapeDtypeStruct((n, d), F32), jax.ShapeDtypeStruct((2, n, 2 * dff), F32)]
    return pl.pallas_call(
        functools.partial(_spost_kernel, dff=dff, final=final),
        grid=(1,), in_specs=in_specs, out_specs=[full(s.shape) for s in out_shape], out_shape=out_shape,
        name=f"sample_post_{l}",
        compiler_params=pltpu.CompilerParams(
            dimension_semantics=("arbitrary",), vmem_limit_bytes=VMEM_LIMIT),
    )(x, att, g0, mp, w['w_branch'], w['w_o'], w['norm2'], w['w_up'], w['fconv_w'], w['fconv_b'],
      w['w_down'], w['norm_f'], fst)


def _cache_mean_kernel(pt_ref, cache_ref, o_ref, buf, sem, *, n_b, n_chunks, n_pages):
    step = pl.program_id(0)
    n_steps = pl.num_programs(0)

    def copies(s, slot):
        l = s // (n_b * n_chunks)
        b = (s // n_chunks) % n_b
        c = s % n_chunks
        out = []
        for p in range(MEAN_PAGES):
            page = pt_ref[b * n_pages + c * MEAN_PAGES + p]
            out.append(pltpu.make_async_copy(cache_ref.at[l, page], buf.at[slot, p], sem.at[slot, p]))
        return out

    @pl.when(step == 0)
    def _():
        for cp in copies(step, 0):
            cp.start()

    slot = step % 2

    @pl.when(step + 1 < n_steps)
    def _():
        for cp in copies(step + 1, 1 - slot):
            cp.start()

    for cp in copies(step, slot):
        cp.wait()
    rows = PAGE_SIZE * N_KV_HEADS
    for j in range(MEAN_PAGES // PAGES_PER_BLOCK):
        tot = jnp.zeros((SUBLANES, HEAD_DIM), F32)
        for p in range(PAGES_PER_BLOCK):
            page = buf.at[slot, PAGES_PER_BLOCK * j + p]
            for g in range(rows // SUBLANES):
                tot = tot + page[g * SUBLANES:(g + 1) * SUBLANES, :]
        tot = (tot[0:N_KV_HEADS] + tot[N_KV_HEADS:2 * N_KV_HEADS]) * (1.0 / MOBA_BLOCK)
        for h in range(N_KV_HEADS):
            o_ref[h, j:j + 1, :] = tot[h:h + 1, :]


def _cache_block_means(cache_k, page_table):
    depth, n_pool, page, hkv, hd = cache_k.shape
    n_b, n_pages = page_table.shape
    n_chunks = n_pages // MEAN_PAGES
    blocks_per_chunk = MEAN_PAGES // PAGES_PER_BLOCK
    grid_spec = pltpu.PrefetchScalarGridSpec(
        num_scalar_prefetch=1,
        grid=(depth * n_b * n_chunks,),
        in_specs=[pl.BlockSpec(memory_space=pl.ANY)],
        out_specs=pl.BlockSpec(
            (None, None, hkv, blocks_per_chunk, hd),
            lambda s, pt: (s // (n_b * n_chunks), (s // n_chunks) % n_b, 0, s % n_chunks, 0)),
        scratch_shapes=[pltpu.VMEM((2, MEAN_PAGES, page * hkv, hd), F32),
                        pltpu.SemaphoreType.DMA((2, MEAN_PAGES))],
    )
    return pl.pallas_call(
        functools.partial(_cache_mean_kernel, n_b=n_b, n_chunks=n_chunks, n_pages=n_pages),
        grid_spec=grid_spec,
        out_shape=jax.ShapeDtypeStruct((depth, n_b, hkv, n_pages // PAGES_PER_BLOCK, hd), F32),
        name="cache_block_means",
        compiler_params=pltpu.CompilerParams(
            dimension_semantics=("arbitrary",), vmem_limit_bytes=VMEM_LIMIT),
    )(page_table.reshape(-1), cache_k.reshape(depth, n_pool, page * hkv, hd))


def _choose_kernel(q_ref, km_ref, o_ref, *, nb):
    lane = lax.broadcasted_iota(jnp.int32, (nb, HEAD_DIM), 1)
    gate = jnp.zeros((nb, HEAD_DIM), F32)
    for h in range(N_HEADS):
        gh = jnp.sum(km_ref[h // GROUP] * q_ref[h:h + 1, :], axis=1, keepdims=True)
        gate = jnp.where(lane == h, gh, gate)
    blk = lax.broadcasted_iota(jnp.int32, (nb, HEAD_DIM), 0)
    row = lax.broadcasted_iota(jnp.int32, (SUBLANES, HEAD_DIM), 0)
    out = jnp.zeros((SUBLANES, HEAD_DIM), jnp.int32)
    for r in range(MOBA_TOPK):
        best = jnp.max(gate, axis=0, keepdims=True)
        idx = jnp.min(jnp.where(gate == best, blk, nb), axis=0, keepdims=True)
        out = jnp.where(row == r, idx, out)
        gate = jnp.where(blk == idx, -jnp.inf, gate)
    o_ref[...] = out


def _sample_choose(q, kmean_l):
    n = q.shape[0]
    nb = kmean_l.shape[2]
    return pl.pallas_call(
        functools.partial(_choose_kernel, nb=nb),
        grid=(n,),
        in_specs=[pl.BlockSpec((None, N_HEADS, HEAD_DIM), lambda b: (b, 0, 0)),
                  pl.BlockSpec((None, N_KV_HEADS, nb, HEAD_DIM), lambda b: (b, 0, 0, 0))],
        out_specs=pl.BlockSpec((None, SUBLANES, HEAD_DIM), lambda b: (b, 0, 0)),
        out_shape=jax.ShapeDtypeStruct((n, SUBLANES, HEAD_DIM), jnp.int32),
        name="sample_choose",
        compiler_params=pltpu.CompilerParams(dimension_semantics=("arbitrary",)),
    )(q.reshape(n, N_HEADS, HEAD_DIM), kmean_l)


def _sattn_kernel(sel_ref, pt_ref, q_ref, kn_ref, vn_ref, ck_ref, cv_ref, o_ref, kbuf, vbuf, sem,
                  *, l, n_pages):
    b = pl.program_id(0)

    def copies(h):
        kv = h // GROUP
        out = []
        for r in range(MOBA_TOPK):
            blk = sel_ref[(b * MOBA_TOPK + r) * N_HEADS + h]
            for p in range(PAGES_PER_BLOCK):
                page = pt_ref[b * n_pages + blk * PAGES_PER_BLOCK + p]
                slot = r * PAGES_PER_BLOCK + p
                for c, (src, dst) in enumerate(((ck_ref, kbuf), (cv_ref, vbuf))):
                    out.append(pltpu.make_async_copy(
                        src.at[l, page, :, kv, :],
                        dst.at[h, pl.ds(slot * PAGE_SIZE, PAGE_SIZE), :],
                        sem.at[c, h, slot]))
        return out

    for h in range(N_HEADS):
        for cp in copies(h):
            cp.start()
    for h in range(N_HEADS):
        kv = h // GROUP
        for cp in copies(h):
            cp.wait()
        q = q_ref[h:h + 1, :]
        s = jnp.sum(kbuf[h] * q, axis=1, keepdims=True) * ATT_SCALE
        s_new = jnp.sum(kn_ref[kv:kv + 1, :] * q, axis=1, keepdims=True) * ATT_SCALE
        m = jnp.maximum(jnp.max(s, axis=0, keepdims=True), s_new)
        p = jnp.exp(s - m)
        p_new = jnp.exp(s_new - m)
        den = jnp.sum(p, axis=0, keepdims=True) + p_new
        num = jnp.sum(p * vbuf[h], axis=0, keepdims=True) + p_new * vn_ref[kv:kv + 1, :]
        o_ref[h:h + 1, :] = num / den


def _sample_attn(l, sel, page_table, q, k_new, v_new, cache_k, cache_v):
    n = q.shape[0]
    n_pages = page_table.shape[1]
    rows = MOBA_TOPK * MOBA_BLOCK
    grid_spec = pltpu.PrefetchScalarGridSpec(
        num_scalar_prefetch=2,
        grid=(n,),
        in_specs=[pl.BlockSpec((None, N_HEADS, HEAD_DIM), lambda b, *_: (b, 0, 0)),
                  pl.BlockSpec((None, N_KV_HEADS, HEAD_DIM), lambda b, *_: (b, 0, 0)),
                  pl.BlockSpec((None, N_KV_HEADS, HEAD_DIM), lambda b, *_: (b, 0, 0)),
                  pl.BlockSpec(memory_space=pl.ANY),
                  pl.BlockSpec(memory_space=pl.ANY)],
        out_specs=pl.BlockSpec((None, N_HEADS, HEAD_DIM), lambda b, *_: (b, 0, 0)),
        scratch_shapes=[pltpu.VMEM((N_HEADS, rows, HEAD_DIM), F32),
                        pltpu.VMEM((N_HEADS, rows, HEAD_DIM), F32),
                        pltpu.SemaphoreType.DMA((2, N_HEADS, MOBA_TOPK * PAGES_PER_BLOCK))],
    )
    sel_flat = sel[:, :MOBA_TOPK, :N_HEADS].reshape(-1)
    out = pl.pallas_call(
        functools.partial(_sattn_kernel, l=l, n_pages=n_pages),
        grid_spec=grid_spec,
        out_shape=jax.ShapeDtypeStruct((n, N_HEADS, HEAD_DIM), F32),
        name=f"sample_attn_{l}",
        compiler_params=pltpu.CompilerParams(dimension_semantics=("arbitrary",)),
    )(sel_flat, page_table.reshape(-1), q.reshape(n, N_HEADS, HEAD_DIM),
      k_new.reshape(n, N_KV_HEADS, HEAD_DIM), v_new.reshape(n, N_KV_HEADS, HEAD_DIM), cache_k, cache_v)
    return out.reshape(n, N_HEADS * HEAD_DIM)


def _trunk(x_prompt, x_sample, cache_k, cache_v, state_sconv, state_rconv, state_lru, state_fconv,
           page_table, w, tm):
    depth = w['w_in'].shape[0]
    bsz, t, d = x_prompt.shape
    n = x_sample.shape[0]
    xp = x_prompt
    xs = x_sample.reshape(n, d)
    sst = jnp.swapaxes(state_sconv, 1, 2)
    rst = jnp.swapaxes(state_rconv, 1, 2)
    fst = jnp.swapaxes(state_fconv, 1, 2)
    kmean_s = _cache_block_means(cache_k, page_table)

    st_p, st_s = [], []
    for l in range(depth):
        final = l == depth - 1
        q, kb, vb, kf, vf, km, g0, mp, scn, rcn, hl = _prompt_mix(l, xp, w, tm)
        att = _prompt_attn(q, kb, vb, km.reshape(bsz, t // MOBA_BLOCK, KV_W))
        xp, fcn = _prompt_post(l, xp, att, g0, mp, w, tm, final)
        st_p.append((kf.reshape(bsz, t, N_KV_HEADS, HEAD_DIM), vf.reshape(bsz, t, N_KV_HEADS, HEAD_DIM),
                     scn, rcn, hl.reshape(bsz, d), fcn))

        qs, ks, vs, g0s, mps, scns, rcns, hls = _sample_mix(l, xs, sst, rst, state_lru, w)
        sel = _sample_choose(qs, kmean_s[l])
        atts = _sample_attn(l, sel, page_table, qs, ks, vs, cache_k, cache_v)
        xs, fcns = _sample_post(l, xs, atts, g0s, mps, fst, w, final)
        st_s.append((ks.reshape(n, 1, N_KV_HEADS, HEAD_DIM), vs.reshape(n, 1, N_KV_HEADS, HEAD_DIM),
                     jnp.swapaxes(scns, 0, 1), jnp.swapaxes(rcns, 0, 1), hls, jnp.swapaxes(fcns, 0, 1)))

    outs_p = [jnp.stack([s[i] for s in st_p], axis=0) for i in range(6)]
    outs_s = [jnp.stack([s[i] for s in st_s], axis=0) for i in range(6)]
    return (xp, xs.reshape(n, 1, d), *outs_p, *outs_s)


def kernel(x_prompt, x_sample, cache_k, cache_v, state_sconv, state_rconv, state_lru, state_fconv, page_table, norm1, w_in, sconv_w, rconv_w, rconv_b, lru_wa, lru_ba, lru_wx, lru_bx, lru_lambda, w_branch, w_merge, b_merge, w_o, norm2, w_up, fconv_w, fconv_b, w_down, norm_f):
    row = lambda a: a.reshape(a.shape[0], 1, a.shape[1])
    w = dict(
        norm1=row(norm1), w_in=w_in.astype(BF16), sconv_w=sconv_w, rconv_w=rconv_w, rconv_b=row(rconv_b),
        lru_wa=lru_wa.astype(BF16), lru_ba=row(lru_ba), lru_wx=lru_wx.astype(BF16), lru_bx=row(lru_bx),
        lru_lambda=row(lru_lambda), w_branch=w_branch.astype(BF16), w_merge=w_merge.astype(BF16),
        b_merge=row(b_merge), w_o=w_o.astype(BF16), norm2=row(norm2), w_up=w_up.astype(BF16),
        fconv_w=fconv_w, fconv_b=row(fconv_b), w_down=w_down.astype(BF16), norm_f=norm_f.reshape(1, -1),
    )
    return _trunk(x_prompt, x_sample, cache_k, cache_v, state_sconv, state_rconv, state_lru, state_fconv,
                  page_table, w, TIME_TILE)
```

```python
import functools
import math

import jax
import jax.numpy as jnp
from jax import lax
from jax.experimental import pallas as pl
from jax.experimental.pallas import tpu as pltpu

F32 = jnp.float32
BF16 = jnp.bfloat16

HEAD_DIM = 128
N_HEADS = 8
N_KV_HEADS = 4
GROUP = N_HEADS // N_KV_HEADS
KV_W = N_KV_HEADS * HEAD_DIM
MOBA_BLOCK = 256
MOBA_TOPK = 3
LRU_BLOCKS = 8
LRU_C = 8.0
RMS_EPS = 1e-6
PAGE_SIZE = 128
PAGES_PER_BLOCK = MOBA_BLOCK // PAGE_SIZE
ATT_SCALE = HEAD_DIM ** -0.5

SUBLANES = 8
HALO = SUBLANES
VMEM_LIMIT = 56 * 1024 * 1024
POST_VMEM_LIMIT = 60 * 1024 * 1024
TIME_TILE = 256


def _dot(a, b):
    return jnp.dot(a, b, preferred_element_type=F32)


def _dot_nt(a, b, precision=None):
    return lax.dot_general(a, b, (((1,), (1,)), ((), ())), preferred_element_type=F32,
                           precision=precision)


def _dot_tn(a, b):
    return lax.dot_general(a, b, (((0,), (0,)), ((), ())), preferred_element_type=F32)


def _rms(x, g):
    return x * lax.rsqrt(jnp.mean(x * x, axis=-1, keepdims=True) + RMS_EPS) * g


def _gelu_tanh(x):
    c = math.sqrt(2.0 / math.pi)
    return x * (0.5 * (1.0 + jnp.tanh(c * (x + 0.044715 * (x * x * x)))))


def _softplus(x):
    return jnp.maximum(x, 0.0) + jnp.log1p(jnp.exp(-jnp.abs(x)))


def _lru_coeffs(xc, r_lin, i_lin, lam):
    r = jax.nn.sigmoid(r_lin)
    i = jax.nn.sigmoid(i_lin)
    log_a = -LRU_C * r * _softplus(-lam)
    a = jnp.exp(log_a)
    th = jnp.tanh(log_a)
    b = jnp.sqrt(-2.0 * th / (1.0 - th)) * (i * xc)
    return a, b


def _const_spec(shape, index):
    return pl.BlockSpec(shape, lambda *_: index, pipeline_mode=pl.Buffered(1))


def _mix_kernel(x_ref, g1_ref, win_ref, scw_ref, rcw_ref, rcb_ref, wax_ref, ba_ref, bx_ref,
                lam_ref, wb1_ref, wb2_ref, wm_ref, bm_ref,
                q_ref, kb_ref, vb_ref, kf_ref, vf_ref, km_ref, g0_ref, mp_ref, scn_ref, rcn_ref, hl_ref,
                ext_sc, ext_rc, lin_a, lin_i, h_carry, *, tm, d):
    t = pl.program_id(1)
    last = pl.num_programs(1) - 1

    @pl.when(t == 0)
    def _():
        ext_sc[0:HALO, :] = jnp.zeros((HALO, d), F32)
        ext_rc[0:HALO, :] = jnp.zeros((HALO, d), F32)
        h_carry[...] = jnp.zeros((1, d), F32)

    xn = _rms(x_ref[...], g1_ref[...]).astype(BF16)
    c0 = d + 2 * KV_W
    bw = d // LRU_BLOCKS

    ext_rc[HALO:HALO + tm, :] = _dot(xn, win_ref[:, c0 + 3 * d:c0 + 4 * d])
    u = _dot(xn, win_ref[:, c0 + d:c0 + 2 * d]) * _dot(xn, win_ref[:, c0 + 2 * d:c0 + 3 * d])
    xc = (rcw_ref[0:1, :] * ext_rc[HALO - 3:HALO - 3 + tm, :]
          + rcw_ref[1:2, :] * ext_rc[HALO - 2:HALO - 2 + tm, :]
          + rcw_ref[2:3, :] * ext_rc[HALO - 1:HALO - 1 + tm, :]
          + rcw_ref[3:4, :] * ext_rc[HALO:HALO + tm, :]) + rcb_ref[...]
    xcb = xc.astype(BF16)
    for n in range(LRU_BLOCKS):
        both = _dot(xcb[:, n * bw:(n + 1) * bw], wax_ref[n])
        lin_a[:, n * bw:(n + 1) * bw] = both[:, 0:bw]
        lin_i[:, n * bw:(n + 1) * bw] = both[:, bw:2 * bw]

    b_gate = _dot(xn, win_ref[:, c0:c0 + d])
    q_ref[...] = _dot(xn, win_ref[:, 0:d]).astype(BF16)
    ext_sc[HALO:HALO + tm, :] = u
    sc = (scw_ref[0:1, :] * ext_sc[HALO - 2:HALO - 2 + tm, :]
          + scw_ref[1:2, :] * ext_sc[HALO - 1:HALO - 1 + tm, :]
          + scw_ref[2:3, :] * u)
    o_sc = (b_gate * sc).astype(BF16)

    a, b = _lru_coeffs(xc, lin_a[...] + ba_ref[...], lin_i[...] + bx_ref[...], lam_ref[...])
    lin_a[...] = a
    lin_i[...] = b

    k = _dot(xn, win_ref[:, d:d + KV_W])
    v = _dot(xn, win_ref[:, d + KV_W:d + 2 * KV_W])
    kb_ref[...] = k.astype(BF16)
    vb_ref[...] = v.astype(BF16)
    for h in range(N_KV_HEADS):
        kf_ref[pl.ds(h, tm, stride=N_KV_HEADS), :] = k[:, h * HEAD_DIM:(h + 1) * HEAD_DIM]
        vf_ref[pl.ds(h, tm, stride=N_KV_HEADS), :] = v[:, h * HEAD_DIM:(h + 1) * HEAD_DIM]
    for j in range(tm // MOBA_BLOCK):
        km_ref[j:j + 1, :] = jnp.mean(k[j * MOBA_BLOCK:(j + 1) * MOBA_BLOCK], axis=0, keepdims=True)
    g0_ref[...] = jax.nn.sigmoid(_dot(xn, wm_ref[:, 0:d]) + bm_ref[:, 0:d])

    row = lax.broadcasted_iota(jnp.int32, (SUBLANES, d), 0)
    h_prev = jnp.broadcast_to(h_carry[...], (SUBLANES, d))
    for g in range(tm // SUBLANES):
        rows = slice(g * SUBLANES, (g + 1) * SUBLANES)
        ag = lin_a[rows, :]
        bg = lin_i[rows, :]
        s = 1
        while s < SUBLANES:
            keep = row >= s
            bg = bg + ag * jnp.where(keep, pltpu.roll(bg, s, 0), 0.0)
            ag = ag * jnp.where(keep, pltpu.roll(ag, s, 0), 1.0)
            s *= 2
        hg = ag * h_prev + bg
        lin_i[rows, :] = hg
        h_prev = jnp.broadcast_to(hg[SUBLANES - 1:SUBLANES, :], (SUBLANES, d))
    h_carry[...] = h_prev[0:1, :]

    pb1 = _dot(o_sc, wb1_ref[...])
    g_lru = _dot(xn, win_ref[:, c0 + 4 * d:c0 + 5 * d])
    mp = jax.nn.sigmoid(_dot(xn, wm_ref[:, d:2 * d]) + bm_ref[:, d:2 * d]) * pb1
    gate2 = jax.nn.sigmoid(_dot(xn, wm_ref[:, 2 * d:3 * d]) + bm_ref[:, 2 * d:3 * d])
    o_lru = (lin_i[...] * _gelu_tanh(g_lru)).astype(BF16)
    mp_ref[...] = mp + gate2 * _dot(o_lru, wb2_ref[...])

    @pl.when(t == last)
    def _():
        scn_ref[...] = ext_sc[HALO + tm - 2:HALO + tm, :]
        rcn_ref[...] = ext_rc[HALO + tm - 3:HALO + tm, :]
        hl_ref[...] = h_carry[...]

    ext_sc[HALO - 2:HALO, :] = ext_sc[HALO + tm - 2:HALO + tm, :]
    ext_rc[HALO - 3:HALO, :] = ext_rc[HALO + tm - 3:HALO + tm, :]


def _prompt_mix(l, x, w, tm):
    bsz, t, d = x.shape
    nt = t // tm
    nmb = tm // MOBA_BLOCK
    n_in = w['w_in'].shape[-1]
    row = lambda b, i: (b, i, 0)
    per_b = lambda b, i: (b, 0, 0)
    vec = lambda n: _const_spec((None, 1, n), (l, 0, 0))
    in_specs = [
        pl.BlockSpec((None, tm, d), row),
        vec(d),
        _const_spec((None, d, n_in), (l, 0, 0)),
        _const_spec((None, 3, d), (l, 0, 0)),
        _const_spec((None, 4, d), (l, 0, 0)),
        vec(d),
        _const_spec((None, LRU_BLOCKS, d // LRU_BLOCKS, 2 * d // LRU_BLOCKS), (l, 0, 0, 0)),
        vec(d),
        vec(d),
        vec(d),
        _const_spec((None, None, d, d), (l, 1, 0, 0)),
        _const_spec((None, None, d, d), (l, 2, 0, 0)),
        _const_spec((None, d, 3 * d), (l, 0, 0)),
        vec(3 * d),
    ]
    out_shape = [
        jax.ShapeDtypeStruct((bsz, t, d), BF16),
        jax.ShapeDtypeStruct((bsz, t, KV_W), BF16),
        jax.ShapeDtypeStruct((bsz, t, KV_W), BF16),
        jax.ShapeDtypeStruct((bsz, t * N_KV_HEADS, HEAD_DIM), F32),
        jax.ShapeDtypeStruct((bsz, t * N_KV_HEADS, HEAD_DIM), F32),
        jax.ShapeDtypeStruct((bsz, nt, nmb, KV_W), F32),
        jax.ShapeDtypeStruct((bsz, t, d), F32),
        jax.ShapeDtypeStruct((bsz, t, d), F32),
        jax.ShapeDtypeStruct((bsz, 2, d), F32),
        jax.ShapeDtypeStruct((bsz, 3, d), F32),
        jax.ShapeDtypeStruct((bsz, 1, d), F32),
    ]
    out_specs = [
        pl.BlockSpec((None, tm, d), row),
        pl.BlockSpec((None, tm, KV_W), row),
        pl.BlockSpec((None, tm, KV_W), row),
        pl.BlockSpec((None, tm * N_KV_HEADS, HEAD_DIM), row),
        pl.BlockSpec((None, tm * N_KV_HEADS, HEAD_DIM), row),
        pl.BlockSpec((None, None, nmb, KV_W), lambda b, i: (b, i, 0, 0)),
        pl.BlockSpec((None, tm, d), row),
        pl.BlockSpec((None, tm, d), row),
        pl.BlockSpec((None, 2, d), per_b),
        pl.BlockSpec((None, 3, d), per_b),
        pl.BlockSpec((None, 1, d), per_b),
    ]
    scratch = [
        pltpu.VMEM((HALO + tm, d), F32),
        pltpu.VMEM((HALO + tm, d), F32),
        pltpu.VMEM((tm, d), F32),
        pltpu.VMEM((tm, d), F32),
        pltpu.VMEM((1, d), F32),
    ]
    return pl.pallas_call(
        functools.partial(_mix_kernel, tm=tm, d=d),
        grid=(bsz, nt), in_specs=in_specs, out_specs=out_specs, out_shape=out_shape,
        scratch_shapes=scratch, name=f"prompt_mix_{l}",
        compiler_params=pltpu.CompilerParams(
            dimension_semantics=("arbitrary", "arbitrary"), vmem_limit_bytes=VMEM_LIMIT),
    )(x, w['norm1'], w['w_in'], w['sconv_w'], w['rconv_w'], w['rconv_b'], w['lru_wax'], w['lru_ba'],
      w['lru_bx'], w['lru_lambda'], w['w_branch'], w['w_branch'], w['w_merge'], w['b_merge'])


def _attn_kernel(q_ref, k_ref, v_ref, km_ref, o_ref, *, nb):
    blk = MOBA_BLOCK
    nq = GROUP * blk

    parts = []
    rest = km_ref[...]
    for _ in range(3):
        part = rest.astype(BF16).astype(F32)
        parts += [part, jnp.zeros_like(part)]
        rest = rest - part
    km_terms = jnp.concatenate(parts, axis=0).astype(BF16)
    prow = 2 * nb

    block_id = lax.broadcasted_iota(jnp.int32, (nb, nq), 0)
    key = lax.broadcasted_iota(jnp.int32, (blk, nq), 0)
    qpos = lax.broadcasted_iota(jnp.int32, (blk, nq), 1) & (blk - 1)
    causal = key <= qpos

    for i in range(nb):
        q_rows = slice(i * blk, (i + 1) * blk)
        n_keys = (i + 1) * blk
        q2 = jnp.concatenate([q_ref[q_rows, g * HEAD_DIM:(g + 1) * HEAD_DIM] for g in range(GROUP)], axis=0)
        if i == 0:
            s_all = _dot_nt(k_ref[0:n_keys, :], q2)
        else:
            s_all = _dot_nt(jnp.concatenate([k_ref[0:n_keys, :], km_terms], axis=0), q2)
            gate = (s_all[n_keys + 2 * prow:n_keys + 2 * prow + nb]
                    + s_all[n_keys + prow:n_keys + prow + nb] + s_all[n_keys:n_keys + nb])
            past = block_id < i
            gate = jnp.where(past, gate, -jnp.inf)
            rank = jnp.zeros((nb, nq), jnp.int32)
            for j in range(i):
                gj = gate[j:j + 1, :]
                rank = rank + jnp.where((gj > gate) | ((gj == gate) & (j < block_id)), 1, 0)
            sel = jnp.where(past & (rank < MOBA_TOPK), 1.0, 0.0)

        pieces = [jnp.where(sel[j:j + 1, :] > 0.5, s_all[j * blk:(j + 1) * blk] * ATT_SCALE, -jnp.inf)
                  for j in range(i)]
        pieces.append(jnp.where(causal, s_all[i * blk:n_keys] * ATT_SCALE, -jnp.inf))
        m = jnp.max(pieces[-1], axis=0, keepdims=True)
        for piece in pieces[:-1]:
            m = jnp.maximum(m, jnp.max(piece, axis=0, keepdims=True))
        probs = [jnp.exp(piece - m) for piece in pieces]
        den = jnp.sum(probs[0], axis=0, keepdims=True)
        for pr in probs[1:]:
            den = den + jnp.sum(pr, axis=0, keepdims=True)
        p_all = jnp.concatenate([pr.astype(BF16) for pr in probs], axis=0)
        out = (_dot_tn(v_ref[0:n_keys, :], p_all) / den).T
        for g in range(GROUP):
            o_ref[q_rows, g * HEAD_DIM:(g + 1) * HEAD_DIM] = out[g * blk:(g + 1) * blk].astype(o_ref.dtype)


def _prompt_attn(q, k, v, kmean):
    bsz, t, d = q.shape
    nb = t // MOBA_BLOCK
    gw = GROUP * HEAD_DIM
    return pl.pallas_call(
        functools.partial(_attn_kernel, nb=nb),
        grid=(bsz, N_KV_HEADS),
        in_specs=[
            pl.BlockSpec((None, t, gw), lambda b, h: (b, 0, h)),
            pl.BlockSpec((None, t, HEAD_DIM), lambda b, h: (b, 0, h)),
            pl.BlockSpec((None, t, HEAD_DIM), lambda b, h: (b, 0, h)),
            pl.BlockSpec((None, nb, HEAD_DIM), lambda b, h: (b, 0, h)),
        ],
        out_specs=pl.BlockSpec((None, t, gw), lambda b, h: (b, 0, h)),
        out_shape=jax.ShapeDtypeStruct((bsz, t, d), BF16),
        name="prompt_attn",
        compiler_params=pltpu.CompilerParams(
            dimension_semantics=("arbitrary", "arbitrary"), vmem_limit_bytes=VMEM_LIMIT),
    )(q, k, v, kmean)


def _post_kernel(pt_ref, x_ref, att_ref, g0_ref, mp_ref, wb0_ref, wo_ref, g2_ref, wup_ref, fcw_ref, fcb_ref,
                 wdn_ref, gf_ref, cache_ref, y_ref, fcn_ref, km_ref, ext_fc, cbuf, csem,
                 *, layer, tm, d, dff, final, pps):
    t = pl.program_id(1)
    last = pl.num_programs(1) - 1
    step = pl.program_id(0) * pl.num_programs(1) + t
    n_steps = pl.num_programs(0) * pl.num_programs(1)
    slot = step % 2

    def page_copies(s, sl):
        return [pltpu.make_async_copy(cache_ref.at[layer, pt_ref[s * pps + p]], cbuf.at[sl, p], csem.at[sl, p])
                for p in range(pps)]

    @pl.when(step == 0)
    def _():
        for cp in page_copies(step, 0):
            cp.start()

    for cp in page_copies(step, slot):
        cp.wait()

    merged = g0_ref[...] * _dot(att_ref[...], wb0_ref[...]) + mp_ref[...]
    x1 = x_ref[...] + _dot(merged.astype(BF16), wo_ref[...])

    for cp in page_copies(jnp.minimum(step + 1, n_steps - 1), 1 - slot):
        cp.start()
    groups = PAGE_SIZE * N_KV_HEADS // SUBLANES
    for j in range(pps // PAGES_PER_BLOCK):
        tot = jnp.zeros((SUBLANES, HEAD_DIM), F32)
        for p in range(PAGES_PER_BLOCK):
            page = cbuf.at[slot, PAGES_PER_BLOCK * j + p]
            for g in range(groups):
                tot = tot + page[g * SUBLANES:(g + 1) * SUBLANES, :]
        tot = (tot[0:N_KV_HEADS] + tot[N_KV_HEADS:2 * N_KV_HEADS]) * (1.0 / MOBA_BLOCK)
        for h in range(N_KV_HEADS):
            km_ref[h, j:j + 1, :] = tot[h:h + 1, :]

    @pl.when(t == 0)
    def _():
        ext_fc[0:HALO, :] = jnp.zeros((HALO, 2 * dff), F32)

    xn2 = _rms(x1, g2_ref[...]).astype(BF16)
    ext_fc[HALO:HALO + tm, :] = _dot(xn2, wup_ref[...])

    def conv(c0, c1):
        return (fcw_ref[0:1, c0:c1] * ext_fc[HALO - 2:HALO - 2 + tm, c0:c1]
                + fcw_ref[1:2, c0:c1] * ext_fc[HALO - 1:HALO - 1 + tm, c0:c1]
                + fcw_ref[2:3, c0:c1] * ext_fc[HALO:HALO + tm, c0:c1]) + fcb_ref[:, c0:c1]

    hh = (jax.nn.silu(conv(0, dff)) * conv(dff, 2 * dff)).astype(BF16)
    x2 = x1 + _dot(hh, wdn_ref[...])
    y_ref[...] = _rms(x2, gf_ref[...]) if final else x2

    @pl.when(t == last)
    def _():
        fcn_ref[...] = ext_fc[HALO + tm - 2:HALO + tm, :]

    ext_fc[HALO - 2:HALO, :] = ext_fc[HALO + tm - 2:HALO + tm, :]

    @pl.when(step == n_steps - 1)
    def _():
        for cp in page_copies(step, 1 - slot):
            cp.wait()


def _prompt_post(l, x, att, g0, mp, w, cache_k, page_table, tm, final):
    bsz, t, d = x.shape
    dff = w['w_down'].shape[1]
    depth, n_pool, page, hkv, hd = cache_k.shape
    n_b, n_pages = page_table.shape
    n_steps = bsz * (t // tm)
    pps = n_b * n_pages // n_steps
    assert pps * n_steps == n_b * n_pages and n_pages % pps == 0
    assert pps % (PAGES_PER_BLOCK * SUBLANES) == 0
    bps = pps // PAGES_PER_BLOCK
    nt = t // tm
    row = lambda b, i, pt: (b, i, 0)
    tile = pl.BlockSpec((None, tm, d), row)
    in_specs = [
        tile, tile, tile, tile,
        _const_spec((None, None, d, d), (l, 0, 0, 0)),
        _const_spec((None, d, d), (l, 0, 0)),
        _const_spec((None, 1, d), (l, 0, 0)),
        _const_spec((None, d, 2 * dff), (l, 0, 0)),
        _const_spec((None, 3, 2 * dff), (l, 0, 0)),
        _const_spec((None, 1, 2 * dff), (l, 0, 0)),
        _const_spec((None, dff, d), (l, 0, 0)),
        _const_spec((1, d), (0, 0)),
        pl.BlockSpec(memory_space=pl.ANY),
    ]
    grid_spec = pltpu.PrefetchScalarGridSpec(
        num_scalar_prefetch=1,
        grid=(bsz, nt),
        in_specs=in_specs,
        out_specs=[tile,
                   pl.BlockSpec((None, 2, 2 * dff), lambda b, i, pt: (b, 0, 0)),
                   pl.BlockSpec((None, hkv, bps, hd), lambda b, i, pt: (b * nt + i, 0, 0, 0))],
        scratch_shapes=[pltpu.VMEM((HALO + tm, 2 * dff), F32),
                        pltpu.VMEM((2, pps, page * hkv, hd), F32),
                        pltpu.SemaphoreType.DMA((2, pps))],
    )
    y, fcn, km = pl.pallas_call(
        functools.partial(_post_kernel, layer=l, tm=tm, d=d, dff=dff, final=final, pps=pps),
        grid_spec=grid_spec,
        out_shape=[jax.ShapeDtypeStruct((bsz, t, d), F32),
                   jax.ShapeDtypeStruct((bsz, 2, 2 * dff), F32),
                   jax.ShapeDtypeStruct((n_steps, hkv, bps, hd), F32)],
        name=f"prompt_post_{l}",
        compiler_params=pltpu.CompilerParams(
            dimension_semantics=("arbitrary", "arbitrary"), vmem_limit_bytes=POST_VMEM_LIMIT),
    )(page_table.reshape(-1), x, att, g0, mp, w['w_branch'], w['w_o'], w['norm2'], w['w_up'], w['fconv_w'],
      w['fconv_b'], w['w_down'], w['norm_f'], cache_k.reshape(depth, n_pool, page * hkv, hd))
    return y, fcn, km.reshape(n_b, n_pages // pps, hkv, bps, hd)


def _smix_kernel(x_ref, g1_ref, win_ref, scw_ref, rcw_ref, rcb_ref, wa_ref, ba_ref, wx_ref, bx_ref,
                 lam_ref, wb1_ref, wb2_ref, wm_ref, bm_ref, sst_ref, rst_ref, h0_ref,
                 q_ref, k_ref, v_ref, g0_ref, mp_ref, scn_ref, rcn_ref, hl_ref, *, d):
    xn = _rms(x_ref[...], g1_ref[...]).astype(BF16)
    q_ref[...] = _dot(xn, win_ref[:, 0:d])
    k_ref[...] = _dot(xn, win_ref[:, d:d + KV_W])
    v_ref[...] = _dot(xn, win_ref[:, d + KV_W:d + 2 * KV_W])
    c0 = d + 2 * KV_W

    u = _dot(xn, win_ref[:, c0 + d:c0 + 2 * d]) * _dot(xn, win_ref[:, c0 + 2 * d:c0 + 3 * d])
    sc = scw_ref[0:1, :] * sst_ref[0] + scw_ref[1:2, :] * sst_ref[1] + scw_ref[2:3, :] * u
    scn_ref[0] = sst_ref[1]
    scn_ref[1] = u
    o_sc = (_dot(xn, win_ref[:, c0:c0 + d]) * sc).astype(BF16)
    pb1 = _dot(o_sc, wb1_ref[...])

    x_lru = _dot(xn, win_ref[:, c0 + 3 * d:c0 + 4 * d])
    xc = (rcw_ref[0:1, :] * rst_ref[0] + rcw_ref[1:2, :] * rst_ref[1] + rcw_ref[2:3, :] * rst_ref[2]
          + rcw_ref[3:4, :] * x_lru) + rcb_ref[...]
    rcn_ref[0] = rst_ref[1]
    rcn_ref[1] = rst_ref[2]
    rcn_ref[2] = x_lru
    bw = d // LRU_BLOCKS
    xcb = xc.astype(BF16)
    r_lin = jnp.concatenate(
        [_dot(xcb[:, n * bw:(n + 1) * bw], wa_ref[n]) for n in range(LRU_BLOCKS)], axis=1)
    i_lin = jnp.concatenate(
        [_dot(xcb[:, n * bw:(n + 1) * bw], wx_ref[n]) for n in range(LRU_BLOCKS)], axis=1)
    a, b = _lru_coeffs(xc, r_lin + ba_ref[...], i_lin + bx_ref[...], lam_ref[...])
    h = a * h0_ref[...] + b
    hl_ref[...] = h
    o_lru = (h * _gelu_tanh(_dot(xn, win_ref[:, c0 + 4 * d:c0 + 5 * d]))).astype(BF16)
    pb2 = _dot(o_lru, wb2_ref[...])

    g0_ref[...] = jax.nn.sigmoid(_dot(xn, wm_ref[:, 0:d]) + bm_ref[:, 0:d])
    mp_ref[...] = (jax.nn.sigmoid(_dot(xn, wm_ref[:, d:2 * d]) + bm_ref[:, d:2 * d]) * pb1
                   + jax.nn.sigmoid(_dot(xn, wm_ref[:, 2 * d:3 * d]) + bm_ref[:, 2 * d:3 * d]) * pb2)


def _sample_mix(l, x, sst, rst, h0, w):
    n, d = x.shape
    n_in = w['w_in'].shape[-1]
    full = lambda shape: pl.BlockSpec(shape, lambda i: (0,) * len(shape))
    lay = lambda shape, idx: pl.BlockSpec(shape, lambda i: idx)
    vec = lambda m: lay((None, 1, m), (l, 0, 0))
    lw = (None, LRU_BLOCKS, d // LRU_BLOCKS, d // LRU_BLOCKS)
    in_specs = [
        full((n, d)), vec(d), lay((None, d, n_in), (l, 0, 0)), lay((None, 3, d), (l, 0, 0)),
        lay((None, 4, d), (l, 0, 0)), vec(d), lay(lw, (l, 0, 0, 0)), vec(d), lay(lw, (l, 0, 0, 0)),
        vec(d), vec(d), lay((None, None, d, d), (l, 1, 0, 0)), lay((None, None, d, d), (l, 2, 0, 0)),
        lay((None, d, 3 * d), (l, 0, 0)), vec(3 * d),
        lay((None, 2, n, d), (l, 0, 0, 0)), lay((None, 3, n, d), (l, 0, 0, 0)), lay((None, n, d), (l, 0, 0)),
    ]
    out_shape = [
        jax.ShapeDtypeStruct((n, d), F32), jax.ShapeDtypeStruct((n, KV_W), F32),
        jax.ShapeDtypeStruct((n, KV_W), F32), jax.ShapeDtypeStruct((n, d), F32),
        jax.ShapeDtypeStruct((n, d), F32), jax.ShapeDtypeStruct((2, n, d), F32),
        jax.ShapeDtypeStruct((3, n, d), F32), jax.ShapeDtypeStruct((n, d), F32),
    ]
    out_specs = [full(s.shape) for s in out_shape]
    return pl.pallas_call(
        functools.partial(_smix_kernel, d=d),
        grid=(1,), in_specs=in_specs, out_specs=out_specs, out_shape=out_shape,
        name=f"sample_mix_{l}",
        compiler_params=pltpu.CompilerParams(
            dimension_semantics=("arbitrary",), vmem_limit_bytes=VMEM_LIMIT),
    )(x, w['norm1'], w['w_in'], w['sconv_w'], w['rconv_w'], w['rconv_b'], w['lru_wa'], w['lru_ba'],
      w['lru_wx'], w['lru_bx'], w['lru_lambda'], w['w_branch'], w['w_branch'], w['w_merge'],
      w['b_merge'], sst, rst, h0)


def _spost_kernel(x_ref, att_ref, g0_ref, mp_ref, wb0_ref, wo_ref, g2_ref, wup_ref, fcw_ref, fcb_ref,
                  wdn_ref, gf_ref, fst_ref, y_ref, fcn_ref, *, dff, final):
    merged = g0_ref[...] * _dot(att_ref[...].astype(BF16), wb0_ref[...]) + mp_ref[...]
    x1 = x_ref[...] + _dot(merged.astype(BF16), wo_ref[...])
    xn2 = _rms(x1, g2_ref[...]).astype(BF16)
    up = _dot(xn2, wup_ref[...])
    c = (fcw_ref[0:1, :] * fst_ref[0] + fcw_ref[1:2, :] * fst_ref[1] + fcw_ref[2:3, :] * up) + fcb_ref[...]
    fcn_ref[0] = fst_ref[1]
    fcn_ref[1] = up
    hh = (jax.nn.silu(c[:, 0:dff]) * c[:, dff:2 * dff]).astype(BF16)
    x2 = x1 + _dot(hh, wdn_ref[...])
    y_ref[...] = _rms(x2, gf_ref[...]) if final else x2


def _sample_post(l, x, att, g0, mp, fst, w, final):
    n, d = x.shape
    dff = w['w_down'].shape[1]
    full = lambda shape: pl.BlockSpec(shape, lambda i: (0,) * len(shape))
    lay = lambda shape, idx: pl.BlockSpec(shape, lambda i: idx)
    in_specs = [
        full((n, d)), full((n, d)), full((n, d)), full((n, d)),
        lay((None, None, d, d), (l, 0, 0, 0)), lay((None, d, d), (l, 0, 0)), lay((None, 1, d), (l, 0, 0)),
        lay((None, d, 2 * dff), (l, 0, 0)), lay((None, 3, 2 * dff), (l, 0, 0)),
        lay((None, 1, 2 * dff), (l, 0, 0)), lay((None, dff, d), (l, 0, 0)), full((1, d)),
        lay((None, 2, n, 2 * dff), (l, 0, 0, 0)),
    ]
    out_shape = [jax.ShapeDtypeStruct((n, d), F32), jax.ShapeDtypeStruct((2, n, 2 * dff), F32)]
    return pl.pallas_call(
        functools.partial(_spost_kernel, dff=dff, final=final),
        grid=(1,), in_specs=in_specs, out_specs=[full(s.shape) for s in out_shape], out_shape=out_shape,
        name=f"sample_post_{l}",
        compiler_params=pltpu.CompilerParams(
            dimension_semantics=("arbitrary",), vmem_limit_bytes=VMEM_LIMIT),
    )(x, att, g0, mp, w['w_branch'], w['w_o'], w['norm2'], w['w_up'], w['fconv_w'], w['fconv_b'],
      w['w_down'], w['norm_f'], fst)


def _choose_kernel(q_ref, km_ref, o_ref, *, nb):
    lane = lax.broadcasted_iota(jnp.int32, (nb, HEAD_DIM), 1)
    gate = jnp.zeros((nb, HEAD_DIM), F32)
    for h in range(N_HEADS):
        gh = jnp.concatenate(
            [jnp.sum(km_ref[part, h // GROUP] * q_ref[h:h + 1, :], axis=1, keepdims=True)
             for part in range(km_ref.shape[0])], axis=0)
        gate = jnp.where(lane == h, gh, gate)
    blk = lax.broadcasted_iota(jnp.int32, (nb, HEAD_DIM), 0)
    row = lax.broadcasted_iota(jnp.int32, (SUBLANES, HEAD_DIM), 0)
    out = jnp.zeros((SUBLANES, HEAD_DIM), jnp.int32)
    for r in range(MOBA_TOPK):
        best = jnp.max(gate, axis=0, keepdims=True)
        idx = jnp.min(jnp.where(gate == best, blk, nb), axis=0, keepdims=True)
        out = jnp.where(row == r, idx, out)
        gate = jnp.where(blk == idx, -jnp.inf, gate)
    o_ref[...] = out


def _sample_choose(q, kmean_l):
    n = q.shape[0]
    parts, bps = kmean_l.shape[1], kmean_l.shape[3]
    return pl.pallas_call(
        functools.partial(_choose_kernel, nb=parts * bps),
        grid=(n,),
        in_specs=[pl.BlockSpec((None, N_HEADS, HEAD_DIM), lambda b: (b, 0, 0)),
                  pl.BlockSpec((None, parts, N_KV_HEADS, bps, HEAD_DIM), lambda b: (b, 0, 0, 0, 0))],
        out_specs=pl.BlockSpec((None, SUBLANES, HEAD_DIM), lambda b: (b, 0, 0)),
        out_shape=jax.ShapeDtypeStruct((n, SUBLANES, HEAD_DIM), jnp.int32),
        name="sample_choose",
        compiler_params=pltpu.CompilerParams(dimension_semantics=("arbitrary",)),
    )(q.reshape(n, N_HEADS, HEAD_DIM), kmean_l)


def _sattn_kernel(sel_ref, pt_ref, q_ref, kn_ref, vn_ref, ck_ref, cv_ref, o_ref, kbuf, vbuf, sem,
                  *, l, n_pages):
    b = pl.program_id(0)

    def copies(h):
        kv = h // GROUP
        out = []
        for r in range(MOBA_TOPK):
            blk = sel_ref[(b * MOBA_TOPK + r) * N_HEADS + h]
            for p in range(PAGES_PER_BLOCK):
                page = pt_ref[b * n_pages + blk * PAGES_PER_BLOCK + p]
                slot = r * PAGES_PER_BLOCK + p
                for c, (src, dst) in enumerate(((ck_ref, kbuf), (cv_ref, vbuf))):
                    out.append(pltpu.make_async_copy(
                        src.at[l, page, :, kv, :],
                        dst.at[h, pl.ds(slot * PAGE_SIZE, PAGE_SIZE), :],
                        sem.at[c, h, slot]))
        return out

    for h in range(N_HEADS):
        for cp in copies(h):
            cp.start()
    for h in range(N_HEADS):
        kv = h // GROUP
        for cp in copies(h):
            cp.wait()
        q = q_ref[h:h + 1, :]
        s = jnp.sum(kbuf[h] * q, axis=1, keepdims=True) * ATT_SCALE
        s_new = jnp.sum(kn_ref[kv:kv + 1, :] * q, axis=1, keepdims=True) * ATT_SCALE
        m = jnp.maximum(jnp.max(s, axis=0, keepdims=True), s_new)
        p = jnp.exp(s - m)
        p_new = jnp.exp(s_new - m)
        den = jnp.sum(p, axis=0, keepdims=True) + p_new
        num = jnp.sum(p * vbuf[h], axis=0, keepdims=True) + p_new * vn_ref[kv:kv + 1, :]
        o_ref[h:h + 1, :] = num / den


def _sample_attn(l, sel, page_table, q, k_new, v_new, cache_k, cache_v):
    n = q.shape[0]
    n_pages = page_table.shape[1]
    rows = MOBA_TOPK * MOBA_BLOCK
    grid_spec = pltpu.PrefetchScalarGridSpec(
        num_scalar_prefetch=2,
        grid=(n,),
        in_specs=[pl.BlockSpec((None, N_HEADS, HEAD_DIM), lambda b, *_: (b, 0, 0)),
                  pl.BlockSpec((None, N_KV_HEADS, HEAD_DIM), lambda b, *_: (b, 0, 0)),
                  pl.BlockSpec((None, N_KV_HEADS, HEAD_DIM), lambda b, *_: (b, 0, 0)),
                  pl.BlockSpec(memory_space=pl.ANY),
                  pl.BlockSpec(memory_space=pl.ANY)],
        out_specs=pl.BlockSpec((None, N_HEADS, HEAD_DIM), lambda b, *_: (b, 0, 0)),
        scratch_shapes=[pltpu.VMEM((N_HEADS, rows, HEAD_DIM), F32),
                        pltpu.VMEM((N_HEADS, rows, HEAD_DIM), F32),
                        pltpu.SemaphoreType.DMA((2, N_HEADS, MOBA_TOPK * PAGES_PER_BLOCK))],
    )
    sel_flat = sel[:, :MOBA_TOPK, :N_HEADS].reshape(-1)
    out = pl.pallas_call(
        functools.partial(_sattn_kernel, l=l, n_pages=n_pages),
        grid_spec=grid_spec,
        out_shape=jax.ShapeDtypeStruct((n, N_HEADS, HEAD_DIM), F32),
        name=f"sample_attn_{l}",
        compiler_params=pltpu.CompilerParams(dimension_semantics=("arbitrary",)),
    )(sel_flat, page_table.reshape(-1), q.reshape(n, N_HEADS, HEAD_DIM),
      k_new.reshape(n, N_KV_HEADS, HEAD_DIM), v_new.reshape(n, N_KV_HEADS, HEAD_DIM), cache_k, cache_v)
    return out.reshape(n, N_HEADS * HEAD_DIM)


def _trunk(x_prompt, x_sample, cache_k, cache_v, state_sconv, state_rconv, state_lru, state_fconv,
           page_table, w, tm):
    depth = w['w_in'].shape[0]
    bsz, t, d = x_prompt.shape
    n = x_sample.shape[0]
    xp = x_prompt
    xs = x_sample.reshape(n, d)
    sst = jnp.swapaxes(state_sconv, 1, 2)
    rst = jnp.swapaxes(state_rconv, 1, 2)
    fst = jnp.swapaxes(state_fconv, 1, 2)

    st_p, st_s = [], []
    for l in range(depth):
        final = l == depth - 1
        q, kb, vb, kf, vf, km, g0, mp, scn, rcn, hl = _prompt_mix(l, xp, w, tm)
        att = _prompt_attn(q, kb, vb, km.reshape(bsz, t // MOBA_BLOCK, KV_W))
        xp, fcn, kmean_s = _prompt_post(l, xp, att, g0, mp, w, cache_k, page_table, tm, final)
        st_p.append((kf.reshape(bsz, t, N_KV_HEADS, HEAD_DIM), vf.reshape(bsz, t, N_KV_HEADS, HEAD_DIM),
                     scn, rcn, hl.reshape(bsz, d), fcn))

        qs, ks, vs, g0s, mps, scns, rcns, hls = _sample_mix(l, xs, sst, rst, state_lru, w)
        sel = _sample_choose(qs, kmean_s)
        atts = _sample_attn(l, sel, page_table, qs, ks, vs, cache_k, cache_v)
        xs, fcns = _sample_post(l, xs, atts, g0s, mps, fst, w, final)
        st_s.append((ks.reshape(n, 1, N_KV_HEADS, HEAD_DIM), vs.reshape(n, 1, N_KV_HEADS, HEAD_DIM),
                     jnp.swapaxes(scns, 0, 1), jnp.swapaxes(rcns, 0, 1), hls, jnp.swapaxes(fcns, 0, 1)))

    outs_p = [jnp.stack([s[i] for s in st_p], axis=0) for i in range(6)]
    outs_s = [jnp.stack([s[i] for s in st_s], axis=0) for i in range(6)]
    return (xp, xs.reshape(n, 1, d), *outs_p, *outs_s)


def kernel(x_prompt, x_sample, cache_k, cache_v, state_sconv, state_rconv, state_lru, state_fconv, page_table, norm1, w_in, sconv_w, rconv_w, rconv_b, lru_wa, lru_ba, lru_wx, lru_bx, lru_lambda, w_branch, w_merge, b_merge, w_o, norm2, w_up, fconv_w, fconv_b, w_down, norm_f):
    row = lambda a: a.reshape(a.shape[0], 1, a.shape[1])
    w = dict(
        norm1=row(norm1), w_in=w_in.astype(BF16), sconv_w=sconv_w, rconv_w=rconv_w, rconv_b=row(rconv_b),
        lru_wa=lru_wa.astype(BF16), lru_ba=row(lru_ba), lru_wx=lru_wx.astype(BF16), lru_bx=row(lru_bx),
        lru_wax=jnp.concatenate([lru_wa, lru_wx], axis=-1).astype(BF16),
        lru_lambda=row(lru_lambda), w_branch=w_branch.astype(BF16), w_merge=w_merge.astype(BF16),
        b_merge=row(b_merge), w_o=w_o.astype(BF16), norm2=row(norm2), w_up=w_up.astype(BF16),
        fconv_w=fconv_w, fconv_b=row(fconv_b), w_down=w_down.astype(BF16), norm_f=norm_f.reshape(1, -1),
    )
    return _trunk(x_prompt, x_sample, cache_k, cache_v, state_sconv, state_rconv, state_lru, state_fconv,
                  page_table, w, TIME_TILE)
```

```python
import functools
import math

import jax
import jax.numpy as jnp
from jax import lax
from jax.experimental import pallas as pl
from jax.experimental.pallas import tpu as pltpu

F32 = jnp.float32
BF16 = jnp.bfloat16

HEAD_DIM = 128
N_HEADS = 8
N_KV_HEADS = 4
GROUP = N_HEADS // N_KV_HEADS
KV_W = N_KV_HEADS * HEAD_DIM
MOBA_BLOCK = 256
MOBA_TOPK = 3
LRU_BLOCKS = 8
LRU_C = 8.0
RMS_EPS = 1e-6
PAGE_SIZE = 128
PAGES_PER_BLOCK = MOBA_BLOCK // PAGE_SIZE
ATT_SCALE = HEAD_DIM ** -0.5
LOG2_E = math.log2(math.e)

SUBLANES = 8
HALO = SUBLANES
VMEM_LIMIT = 56 * 1024 * 1024
POST_VMEM_LIMIT = 60 * 1024 * 1024
TIME_TILE = 256


def _dot(a, b):
    return jnp.dot(a, b, preferred_element_type=F32)


def _dot_nt(a, b, precision=None):
    return lax.dot_general(a, b, (((1,), (1,)), ((), ())), preferred_element_type=F32,
                           precision=precision)


def _dot_tn(a, b):
    return lax.dot_general(a, b, (((0,), (0,)), ((), ())), preferred_element_type=F32)


def _rms(x, g):
    return x * lax.rsqrt(jnp.mean(x * x, axis=-1, keepdims=True) + RMS_EPS) * g


def _gelu_tanh(x):
    c = math.sqrt(2.0 / math.pi)
    return x * (0.5 * (1.0 + jnp.tanh(c * (x + 0.044715 * (x * x * x)))))


def _softplus(x):
    return jnp.maximum(x, 0.0) + jnp.log1p(jnp.exp(-jnp.abs(x)))


def _lru_coeffs(xc, r_lin, i_lin, lam):
    r = jax.nn.sigmoid(r_lin)
    i = jax.nn.sigmoid(i_lin)
    log_a = -LRU_C * r * _softplus(-lam)
    a = jnp.exp(log_a)
    th = jnp.tanh(log_a)
    b = jnp.sqrt(-2.0 * th / (1.0 - th)) * (i * xc)
    return a, b


def _const_spec(shape, index):
    return pl.BlockSpec(shape, lambda *_: index, pipeline_mode=pl.Buffered(1))


def _mix_kernel(x_ref, g1_ref, win_ref, scw_ref, rcw_ref, rcb_ref, wax_ref, ba_ref, bx_ref,
                lam_ref, wb1_ref, wb2_ref, wm_ref, bm_ref,
                q_ref, kb_ref, vb_ref, kf_ref, vf_ref, km_ref, g0_ref, mp_ref, scn_ref, rcn_ref, hl_ref,
                ext_sc, ext_rc, lin_a, lin_i, h_carry, *, tm, d):
    t = pl.program_id(1)
    last = pl.num_programs(1) - 1

    @pl.when(t == 0)
    def _():
        ext_sc[0:HALO, :] = jnp.zeros((HALO, d), F32)
        ext_rc[0:HALO, :] = jnp.zeros((HALO, d), F32)
        h_carry[...] = jnp.zeros((1, d), F32)

    xn = _rms(x_ref[...], g1_ref[...]).astype(BF16)
    c0 = d + 2 * KV_W
    bw = d // LRU_BLOCKS

    ext_rc[HALO:HALO + tm, :] = _dot(xn, win_ref[:, c0 + 3 * d:c0 + 4 * d])
    u = _dot(xn, win_ref[:, c0 + d:c0 + 2 * d]) * _dot(xn, win_ref[:, c0 + 2 * d:c0 + 3 * d])
    xc = (rcw_ref[0:1, :] * ext_rc[HALO - 3:HALO - 3 + tm, :]
          + rcw_ref[1:2, :] * ext_rc[HALO - 2:HALO - 2 + tm, :]
          + rcw_ref[2:3, :] * ext_rc[HALO - 1:HALO - 1 + tm, :]
          + rcw_ref[3:4, :] * ext_rc[HALO:HALO + tm, :]) + rcb_ref[...]
    xcb = xc.astype(BF16)
    for n in range(LRU_BLOCKS):
        both = _dot(xcb[:, n * bw:(n + 1) * bw], wax_ref[n])
        lin_a[:, n * bw:(n + 1) * bw] = both[:, 0:bw]
        lin_i[:, n * bw:(n + 1) * bw] = both[:, bw:2 * bw]

    b_gate = _dot(xn, win_ref[:, c0:c0 + d])
    q_ref[...] = _dot(xn, win_ref[:, 0:d]).astype(BF16)
    ext_sc[HALO:HALO + tm, :] = u
    sc = (scw_ref[0:1, :] * ext_sc[HALO - 2:HALO - 2 + tm, :]
          + scw_ref[1:2, :] * ext_sc[HALO - 1:HALO - 1 + tm, :]
          + scw_ref[2:3, :] * u)
    o_sc = (b_gate * sc).astype(BF16)

    a, b = _lru_coeffs(xc, lin_a[...] + ba_ref[...], lin_i[...] + bx_ref[...], lam_ref[...])
    lin_a[...] = a
    lin_i[...] = b

    k = _dot(xn, win_ref[:, d:d + KV_W])
    v = _dot(xn, win_ref[:, d + KV_W:d + 2 * KV_W])
    kb_ref[...] = k.astype(BF16)
    vb_ref[...] = v.astype(BF16)
    for h in range(N_KV_HEADS):
        kf_ref[pl.ds(h, tm, stride=N_KV_HEADS), :] = k[:, h * HEAD_DIM:(h + 1) * HEAD_DIM]
        vf_ref[pl.ds(h, tm, stride=N_KV_HEADS), :] = v[:, h * HEAD_DIM:(h + 1) * HEAD_DIM]
    for j in range(tm // MOBA_BLOCK):
        km_ref[j:j + 1, :] = jnp.mean(k[j * MOBA_BLOCK:(j + 1) * MOBA_BLOCK], axis=0, keepdims=True)
    g0_ref[...] = jax.nn.sigmoid(_dot(xn, wm_ref[:, 0:d]) + bm_ref[:, 0:d])

    row = lax.broadcasted_iota(jnp.int32, (SUBLANES, d), 0)
    h_prev = jnp.broadcast_to(h_carry[...], (SUBLANES, d))
    for g in range(tm // SUBLANES):
        rows = slice(g * SUBLANES, (g + 1) * SUBLANES)
        ag = lin_a[rows, :]
        bg = lin_i[rows, :]
        s = 1
        while s < SUBLANES:
            keep = row >= s
            bg = bg + ag * jnp.where(keep, pltpu.roll(bg, s, 0), 0.0)
            ag = ag * jnp.where(keep, pltpu.roll(ag, s, 0), 1.0)
            s *= 2
        hg = ag * h_prev + bg
        lin_i[rows, :] = hg
        h_prev = jnp.broadcast_to(hg[SUBLANES - 1:SUBLANES, :], (SUBLANES, d))
    h_carry[...] = h_prev[0:1, :]

    pb1 = _dot(o_sc, wb1_ref[...])
    g_lru = _dot(xn, win_ref[:, c0 + 4 * d:c0 + 5 * d])
    mp = jax.nn.sigmoid(_dot(xn, wm_ref[:, d:2 * d]) + bm_ref[:, d:2 * d]) * pb1
    gate2 = jax.nn.sigmoid(_dot(xn, wm_ref[:, 2 * d:3 * d]) + bm_ref[:, 2 * d:3 * d])
    o_lru = (lin_i[...] * _gelu_tanh(g_lru)).astype(BF16)
    mp_ref[...] = mp + gate2 * _dot(o_lru, wb2_ref[...])

    @pl.when(t == last)
    def _():
        scn_ref[...] = ext_sc[HALO + tm - 2:HALO + tm, :]
        rcn_ref[...] = ext_rc[HALO + tm - 3:HALO + tm, :]
        hl_ref[...] = h_carry[...]

    ext_sc[HALO - 2:HALO, :] = ext_sc[HALO + tm - 2:HALO + tm, :]
    ext_rc[HALO - 3:HALO, :] = ext_rc[HALO + tm - 3:HALO + tm, :]


def _prompt_mix(l, x, w, tm):
    bsz, t, d = x.shape
    nt = t // tm
    nmb = tm // MOBA_BLOCK
    n_in = w['w_in'].shape[-1]
    row = lambda b, i: (b, i, 0)
    per_b = lambda b, i: (b, 0, 0)
    vec = lambda n: _const_spec((None, 1, n), (l, 0, 0))
    in_specs = [
        pl.BlockSpec((None, tm, d), row),
        vec(d),
        _const_spec((None, d, n_in), (l, 0, 0)),
        _const_spec((None, 3, d), (l, 0, 0)),
        _const_spec((None, 4, d), (l, 0, 0)),
        vec(d),
        _const_spec((None, LRU_BLOCKS, d // LRU_BLOCKS, 2 * d // LRU_BLOCKS), (l, 0, 0, 0)),
        vec(d),
        vec(d),
        vec(d),
        _const_spec((None, None, d, d), (l, 1, 0, 0)),
        _const_spec((None, None, d, d), (l, 2, 0, 0)),
        _const_spec((None, d, 3 * d), (l, 0, 0)),
        vec(3 * d),
    ]
    out_shape = [
        jax.ShapeDtypeStruct((bsz, t, d), BF16),
        jax.ShapeDtypeStruct((bsz, t, KV_W), BF16),
        jax.ShapeDtypeStruct((bsz, t, KV_W), BF16),
        jax.ShapeDtypeStruct((bsz, t * N_KV_HEADS, HEAD_DIM), F32),
        jax.ShapeDtypeStruct((bsz, t * N_KV_HEADS, HEAD_DIM), F32),
        jax.ShapeDtypeStruct((bsz, nt, nmb, KV_W), F32),
        jax.ShapeDtypeStruct((bsz, t, d), F32),
        jax.ShapeDtypeStruct((bsz, t, d), F32),
        jax.ShapeDtypeStruct((bsz, 2, d), F32),
        jax.ShapeDtypeStruct((bsz, 3, d), F32),
        jax.ShapeDtypeStruct((bsz, 1, d), F32),
    ]
    out_specs = [
        pl.BlockSpec((None, tm, d), row),
        pl.BlockSpec((None, tm, KV_W), row),
        pl.BlockSpec((None, tm, KV_W), row),
        pl.BlockSpec((None, tm * N_KV_HEADS, HEAD_DIM), row),
        pl.BlockSpec((None, tm * N_KV_HEADS, HEAD_DIM), row),
        pl.BlockSpec((None, None, nmb, KV_W), lambda b, i: (b, i, 0, 0)),
        pl.BlockSpec((None, tm, d), row),
        pl.BlockSpec((None, tm, d), row),
        pl.BlockSpec((None, 2, d), per_b),
        pl.BlockSpec((None, 3, d), per_b),
        pl.BlockSpec((None, 1, d), per_b),
    ]
    scratch = [
        pltpu.VMEM((HALO + tm, d), F32),
        pltpu.VMEM((HALO + tm, d), F32),
        pltpu.VMEM((tm, d), F32),
        pltpu.VMEM((tm, d), F32),
        pltpu.VMEM((1, d), F32),
    ]
    return pl.pallas_call(
        functools.partial(_mix_kernel, tm=tm, d=d),
        grid=(bsz, nt), in_specs=in_specs, out_specs=out_specs, out_shape=out_shape,
        scratch_shapes=scratch, name=f"prompt_mix_{l}",
        compiler_params=pltpu.CompilerParams(
            dimension_semantics=("arbitrary", "arbitrary"), vmem_limit_bytes=VMEM_LIMIT),
    )(x, w['norm1'], w['w_in'], w['sconv_w'], w['rconv_w'], w['rconv_b'], w['lru_wax'], w['lru_ba'],
      w['lru_bx'], w['lru_lambda'], w['w_branch'], w['w_branch'], w['w_merge'], w['b_merge'])


def _attn_kernel(q_ref, k_ref, v_ref, km_ref, o_ref, *, nb):
    blk = MOBA_BLOCK
    nq = GROUP * blk

    parts = []
    rest = km_ref[...]
    for _ in range(3):
        part = rest.astype(BF16).astype(F32)
        parts += [part, jnp.zeros_like(part)]
        rest = rest - part
    km_terms = jnp.concatenate(parts, axis=0).astype(BF16)
    prow = 2 * nb

    block_id = lax.broadcasted_iota(jnp.int32, (nb, nq), 0)
    key = lax.broadcasted_iota(jnp.int32, (blk, nq), 0)
    qpos = lax.broadcasted_iota(jnp.int32, (blk, nq), 1) & (blk - 1)
    causal = key <= qpos

    def scores(i):
        q2 = jnp.concatenate([q_ref[i * blk:(i + 1) * blk, g * HEAD_DIM:(g + 1) * HEAD_DIM]
                              for g in range(GROUP)], axis=0)
        keys = k_ref[0:(i + 1) * blk, :]
        return _dot_nt(keys if i == 0 else jnp.concatenate([keys, km_terms], axis=0), q2)

    s_next = scores(0)
    for i in range(nb):
        q_rows = slice(i * blk, (i + 1) * blk)
        n_keys = (i + 1) * blk
        s_all = s_next
        if i + 1 < nb:
            s_next = scores(i + 1)
        if i > 0:
            gate = (s_all[n_keys + 2 * prow:n_keys + 2 * prow + nb]
                    + s_all[n_keys + prow:n_keys + prow + nb] + s_all[n_keys:n_keys + nb])
            past = block_id < i
            gate = jnp.where(past, gate, -jnp.inf)
            rank = jnp.zeros((nb, nq), jnp.int32)
            for j in range(i):
                gj = gate[j:j + 1, :]
                rank = rank + jnp.where((gj > gate) | ((gj == gate) & (j < block_id)), 1, 0)
            sel = jnp.where(past & (rank < MOBA_TOPK), 1.0, 0.0)

        pieces = [jnp.where(sel[j:j + 1, :] > 0.5, s_all[j * blk:(j + 1) * blk], -jnp.inf) for j in range(i)]
        pieces.append(jnp.where(causal, s_all[i * blk:n_keys], -jnp.inf))
        m = jnp.max(pieces[-1], axis=0, keepdims=True)
        for piece in pieces[:-1]:
            m = jnp.maximum(m, jnp.max(piece, axis=0, keepdims=True))
        probs = [jnp.exp2((piece - m) * (ATT_SCALE * LOG2_E)) for piece in pieces]
        den = jnp.sum(probs[0], axis=0, keepdims=True)
        for pr in probs[1:]:
            den = den + jnp.sum(pr, axis=0, keepdims=True)
        p_all = jnp.concatenate([pr.astype(BF16) for pr in probs], axis=0)
        out = (_dot_tn(v_ref[0:n_keys, :], p_all) / den).T
        for g in range(GROUP):
            o_ref[q_rows, g * HEAD_DIM:(g + 1) * HEAD_DIM] = out[g * blk:(g + 1) * blk].astype(o_ref.dtype)


def _prompt_attn(q, k, v, kmean):
    bsz, t, d = q.shape
    nb = t // MOBA_BLOCK
    gw = GROUP * HEAD_DIM
    return pl.pallas_call(
        functools.partial(_attn_kernel, nb=nb),
        grid=(bsz, N_KV_HEADS),
        in_specs=[
            pl.BlockSpec((None, t, gw), lambda b, h: (b, 0, h)),
            pl.BlockSpec((None, t, HEAD_DIM), lambda b, h: (b, 0, h)),
            pl.BlockSpec((None, t, HEAD_DIM), lambda b, h: (b, 0, h)),
            pl.BlockSpec((None, nb, HEAD_DIM), lambda b, h: (b, 0, h)),
        ],
        out_specs=pl.BlockSpec((None, t, gw), lambda b, h: (b, 0, h)),
        out_shape=jax.ShapeDtypeStruct((bsz, t, d), BF16),
        name="prompt_attn",
        compiler_params=pltpu.CompilerParams(
            dimension_semantics=("arbitrary", "arbitrary"), vmem_limit_bytes=VMEM_LIMIT),
    )(q, k, v, kmean)


def _post_kernel(pt_ref, x_ref, att_ref, g0_ref, mp_ref, wb0_ref, wo_ref, g2_ref, wup_ref, fcw_ref, fcb_ref,
                 wdn_ref, gf_ref, cache_ref, y_ref, fcn_ref, km_ref, ext_fc, cbuf, csem,
                 *, layer, tm, d, dff, final, pps):
    t = pl.program_id(1)
    last = pl.num_programs(1) - 1
    step = pl.program_id(0) * pl.num_programs(1) + t
    n_steps = pl.num_programs(0) * pl.num_programs(1)
    slot = step % 2

    def page_copies(s, sl):
        return [pltpu.make_async_copy(cache_ref.at[layer, pt_ref[s * pps + p]], cbuf.at[sl, p], csem.at[sl, p])
                for p in range(pps)]

    @pl.when(step == 0)
    def _():
        for cp in page_copies(step, 0):
            cp.start()

    for cp in page_copies(step, slot):
        cp.wait()

    merged = g0_ref[...] * _dot(att_ref[...], wb0_ref[...]) + mp_ref[...]
    x1 = x_ref[...] + _dot(merged.astype(BF16), wo_ref[...])

    for cp in page_copies(jnp.minimum(step + 1, n_steps - 1), 1 - slot):
        cp.start()
    groups = PAGE_SIZE * N_KV_HEADS // SUBLANES
    for j in range(pps // PAGES_PER_BLOCK):
        tot = jnp.zeros((SUBLANES, HEAD_DIM), F32)
        for p in range(PAGES_PER_BLOCK):
            page = cbuf.at[slot, PAGES_PER_BLOCK * j + p]
            for g in range(groups):
                tot = tot + page[g * SUBLANES:(g + 1) * SUBLANES, :]
        tot = (tot[0:N_KV_HEADS] + tot[N_KV_HEADS:2 * N_KV_HEADS]) * (1.0 / MOBA_BLOCK)
        for h in range(N_KV_HEADS):
            km_ref[h, j:j + 1, :] = tot[h:h + 1, :]

    @pl.when(t == 0)
    def _():
        ext_fc[0:HALO, :] = jnp.zeros((HALO, 2 * dff), F32)

    xn2 = _rms(x1, g2_ref[...]).astype(BF16)
    ext_fc[HALO:HALO + tm, :] = _dot(xn2, wup_ref[...])

    def conv(c0, c1):
        return (fcw_ref[0:1, c0:c1] * ext_fc[HALO - 2:HALO - 2 + tm, c0:c1]
                + fcw_ref[1:2, c0:c1] * ext_fc[HALO - 1:HALO - 1 + tm, c0:c1]
                + fcw_ref[2:3, c0:c1] * ext_fc[HALO:HALO + tm, c0:c1]) + fcb_ref[:, c0:c1]

    hh = (jax.nn.silu(conv(0, dff)) * conv(dff, 2 * dff)).astype(BF16)
    x2 = x1 + _dot(hh, wdn_ref[...])
    y_ref[...] = _rms(x2, gf_ref[...]) if final else x2

    @pl.when(t == last)
    def _():
        fcn_ref[...] = ext_fc[HALO + tm - 2:HALO + tm, :]

    ext_fc[HALO - 2:HALO, :] = ext_fc[HALO + tm - 2:HALO + tm, :]

    @pl.when(step == n_steps - 1)
    def _():
        for cp in page_copies(step, 1 - slot):
            cp.wait()


def _prompt_post(l, x, att, g0, mp, w, cache_k, page_table, tm, final):
    bsz, t, d = x.shape
    dff = w['w_down'].shape[1]
    depth, n_pool, page, hkv, hd = cache_k.shape
    n_b, n_pages = page_table.shape
    n_steps = bsz * (t // tm)
    pps = n_b * n_pages // n_steps
    assert pps * n_steps == n_b * n_pages and n_pages % pps == 0
    assert pps % (PAGES_PER_BLOCK * SUBLANES) == 0
    bps = pps // PAGES_PER_BLOCK
    nt = t // tm
    row = lambda b, i, pt: (b, i, 0)
    tile = pl.BlockSpec((None, tm, d), row)
    in_specs = [
        tile, tile, tile, tile,
        _const_spec((None, None, d, d), (l, 0, 0, 0)),
        _const_spec((None, d, d), (l, 0, 0)),
        _const_spec((None, 1, d), (l, 0, 0)),
        _const_spec((None, d, 2 * dff), (l, 0, 0)),
        _const_spec((None, 3, 2 * dff), (l, 0, 0)),
        _const_spec((None, 1, 2 * dff), (l, 0, 0)),
        _const_spec((None, dff, d), (l, 0, 0)),
        _const_spec((1, d), (0, 0)),
        pl.BlockSpec(memory_space=pl.ANY),
    ]
    grid_spec = pltpu.PrefetchScalarGridSpec(
        num_scalar_prefetch=1,
        grid=(bsz, nt),
        in_specs=in_specs,
        out_specs=[tile,
                   pl.BlockSpec((None, 2, 2 * dff), lambda b, i, pt: (b, 0, 0)),
                   pl.BlockSpec((None, hkv, bps, hd), lambda b, i, pt: (b * nt + i, 0, 0, 0))],
        scratch_shapes=[pltpu.VMEM((HALO + tm, 2 * dff), F32),
                        pltpu.VMEM((2, pps, page * hkv, hd), F32),
                        pltpu.SemaphoreType.DMA((2, pps))],
    )
    y, fcn, km = pl.pallas_call(
        functools.partial(_post_kernel, layer=l, tm=tm, d=d, dff=dff, final=final, pps=pps),
        grid_spec=grid_spec,
        out_shape=[jax.ShapeDtypeStruct((bsz, t, d), F32),
                   jax.ShapeDtypeStruct((bsz, 2, 2 * dff), F32),
                   jax.ShapeDtypeStruct((n_steps, hkv, bps, hd), F32)],
        name=f"prompt_post_{l}",
        compiler_params=pltpu.CompilerParams(
            dimension_semantics=("arbitrary", "arbitrary"), vmem_limit_bytes=POST_VMEM_LIMIT),
    )(page_table.reshape(-1), x, att, g0, mp, w['w_branch'], w['w_o'], w['norm2'], w['w_up'], w['fconv_w'],
      w['fconv_b'], w['w_down'], w['norm_f'], cache_k.reshape(depth, n_pool, page * hkv, hd))
    return y, fcn, km.reshape(n_b, n_pages // pps, hkv, bps, hd)


def _smix_kernel(x_ref, g1_ref, win_ref, scw_ref, rcw_ref, rcb_ref, wa_ref, ba_ref, wx_ref, bx_ref,
                 lam_ref, wb1_ref, wb2_ref, wm_ref, bm_ref, sst_ref, rst_ref, h0_ref,
                 q_ref, k_ref, v_ref, g0_ref, mp_ref, scn_ref, rcn_ref, hl_ref, *, d):
    xn = _rms(x_ref[...], g1_ref[...]).astype(BF16)
    q_ref[...] = _dot(xn, win_ref[:, 0:d])
    k_ref[...] = _dot(xn, win_ref[:, d:d + KV_W])
    v_ref[...] = _dot(xn, win_ref[:, d + KV_W:d + 2 * KV_W])
    c0 = d + 2 * KV_W

    u = _dot(xn, win_ref[:, c0 + d:c0 + 2 * d]) * _dot(xn, win_ref[:, c0 + 2 * d:c0 + 3 * d])
    sc = scw_ref[0:1, :] * sst_ref[0] + scw_ref[1:2, :] * sst_ref[1] + scw_ref[2:3, :] * u
    scn_ref[0] = sst_ref[1]
    scn_ref[1] = u
    o_sc = (_dot(xn, win_ref[:, c0:c0 + d]) * sc).astype(BF16)
    pb1 = _dot(o_sc, wb1_ref[...])

    x_lru = _dot(xn, win_ref[:, c0 + 3 * d:c0 + 4 * d])
    xc = (rcw_ref[0:1, :] * rst_ref[0] + rcw_ref[1:2, :] * rst_ref[1] + rcw_ref[2:3, :] * rst_ref[2]
          + rcw_ref[3:4, :] * x_lru) + rcb_ref[...]
    rcn_ref[0] = rst_ref[1]
    rcn_ref[1] = rst_ref[2]
    rcn_ref[2] = x_lru
    bw = d // LRU_BLOCKS
    xcb = xc.astype(BF16)
    r_lin = jnp.concatenate(
        [_dot(xcb[:, n * bw:(n + 1) * bw], wa_ref[n]) for n in range(LRU_BLOCKS)], axis=1)
    i_lin = jnp.concatenate(
        [_dot(xcb[:, n * bw:(n + 1) * bw], wx_ref[n]) for n in range(LRU_BLOCKS)], axis=1)
    a, b = _lru_coeffs(xc, r_lin + ba_ref[...], i_lin + bx_ref[...], lam_ref[...])
    h = a * h0_ref[...] + b
    hl_ref[...] = h
    o_lru = (h * _gelu_tanh(_dot(xn, win_ref[:, c0 + 4 * d:c0 + 5 * d]))).astype(BF16)
    pb2 = _dot(o_lru, wb2_ref[...])

    g0_ref[...] = jax.nn.sigmoid(_dot(xn, wm_ref[:, 0:d]) + bm_ref[:, 0:d])
    mp_ref[...] = (jax.nn.sigmoid(_dot(xn, wm_ref[:, d:2 * d]) + bm_ref[:, d:2 * d]) * pb1
                   + jax.nn.sigmoid(_dot(xn, wm_ref[:, 2 * d:3 * d]) + bm_ref[:, 2 * d:3 * d]) * pb2)


def _sample_mix(l, x, sst, rst, h0, w):
    n, d = x.shape
    n_in = w['w_in'].shape[-1]
    full = lambda shape: pl.BlockSpec(shape, lambda i: (0,) * len(shape))
    lay = lambda shape, idx: pl.BlockSpec(shape, lambda i: idx)
    vec = lambda m: lay((None, 1, m), (l, 0, 0))
    lw = (None, LRU_BLOCKS, d // LRU_BLOCKS, d // LRU_BLOCKS)
    in_specs = [
        full((n, d)), vec(d), lay((None, d, n_in), (l, 0, 0)), lay((None, 3, d), (l, 0, 0)),
        lay((None, 4, d), (l, 0, 0)), vec(d), lay(lw, (l, 0, 0, 0)), vec(d), lay(lw, (l, 0, 0, 0)),
        vec(d), vec(d), lay((None, None, d, d), (l, 1, 0, 0)), lay((None, None, d, d), (l, 2, 0, 0)),
        lay((None, d, 3 * d), (l, 0, 0)), vec(3 * d),
        lay((None, 2, n, d), (l, 0, 0, 0)), lay((None, 3, n, d), (l, 0, 0, 0)), lay((None, n, d), (l, 0, 0)),
    ]
    out_shape = [
        jax.ShapeDtypeStruct((n, d), F32), jax.ShapeDtypeStruct((n, KV_W), F32),
        jax.ShapeDtypeStruct((n, KV_W), F32), jax.ShapeDtypeStruct((n, d), F32),
        jax.ShapeDtypeStruct((n, d), F32), jax.ShapeDtypeStruct((2, n, d), F32),
        jax.ShapeDtypeStruct((3, n, d), F32), jax.ShapeDtypeStruct((n, d), F32),
    ]
    out_specs = [full(s.shape) for s in out_shape]
    return pl.pallas_call(
        functools.partial(_smix_kernel, d=d),
        grid=(1,), in_specs=in_specs, out_specs=out_specs, out_shape=out_shape,
        name=f"sample_mix_{l}",
        compiler_params=pltpu.CompilerParams(
            dimension_semantics=("arbitrary",), vmem_limit_bytes=VMEM_LIMIT),
    )(x, w['norm1'], w['w_in'], w['sconv_w'], w['rconv_w'], w['rconv_b'], w['lru_wa'], w['lru_ba'],
      w['lru_wx'], w['lru_bx'], w['lru_lambda'], w['w_branch'], w['w_branch'], w['w_merge'],
      w['b_merge'], sst, rst, h0)


def _spost_kernel(x_ref, att_ref, g0_ref, mp_ref, wb0_ref, wo_ref, g2_ref, wup_ref, fcw_ref, fcb_ref,
                  wdn_ref, gf_ref, fst_ref, y_ref, fcn_ref, *, dff, final):
    merged = g0_ref[...] * _dot(att_ref[...].astype(BF16), wb0_ref[...]) + mp_ref[...]
    x1 = x_ref[...] + _dot(merged.astype(BF16), wo_ref[...])
    xn2 = _rms(x1, g2_ref[...]).astype(BF16)
    up = _dot(xn2, wup_ref[...])
    c = (fcw_ref[0:1, :] * fst_ref[0] + fcw_ref[1:2, :] * fst_ref[1] + fcw_ref[2:3, :] * up) + fcb_ref[...]
    fcn_ref[0] = fst_ref[1]
    fcn_ref[1] = up
    hh = (jax.nn.silu(c[:, 0:dff]) * c[:, dff:2 * dff]).astype(BF16)
    x2 = x1 + _dot(hh, wdn_ref[...])
    y_ref[...] = _rms(x2, gf_ref[...]) if final else x2


def _sample_post(l, x, att, g0, mp, fst, w, final):
    n, d = x.shape
    dff = w['w_down'].shape[1]
    full = lambda shape: pl.BlockSpec(shape, lambda i: (0,) * len(shape))
    lay = lambda shape, idx: pl.BlockSpec(shape, lambda i: idx)
    in_specs = [
        full((n, d)), full((n, d)), full((n, d)), full((n, d)),
        lay((None, None, d, d), (l, 0, 0, 0)), lay((None, d, d), (l, 0, 0)), lay((None, 1, d), (l, 0, 0)),
        lay((None, d, 2 * dff), (l, 0, 0)), lay((None, 3, 2 * dff), (l, 0, 0)),
        lay((None, 1, 2 * dff), (l, 0, 0)), lay((None, dff, d), (l, 0, 0)), full((1, d)),
        lay((None, 2, n, 2 * dff), (l, 0, 0, 0)),
    ]
    out_shape = [jax.ShapeDtypeStruct((n, d), F32), jax.ShapeDtypeStruct((2, n, 2 * dff), F32)]
    return pl.pallas_call(
        functools.partial(_spost_kernel, dff=dff, final=final),
        grid=(1,), in_specs=in_specs, out_specs=[full(s.shape) for s in out_shape], out_shape=out_shape,
        name=f"sample_post_{l}",
        compiler_params=pltpu.CompilerParams(
            dimension_semantics=("arbitrary",), vmem_limit_bytes=VMEM_LIMIT),
    )(x, att, g0, mp, w['w_branch'], w['w_o'], w['norm2'], w['w_up'], w['fconv_w'], w['fconv_b'],
      w['w_down'], w['norm_f'], fst)


def _choose_kernel(q_ref, km_ref, o_ref, *, nb):
    lane = lax.broadcasted_iota(jnp.int32, (nb, HEAD_DIM), 1)
    gate = jnp.zeros((nb, HEAD_DIM), F32)
    for h in range(N_HEADS):
        gh = jnp.concatenate(
            [jnp.sum(km_ref[part, h // GROUP] * q_ref[h:h + 1, :], axis=1, keepdims=True)
             for part in range(km_ref.shape[0])], axis=0)
        gate = jnp.where(lane == h, gh, gate)
    blk = lax.broadcasted_iota(jnp.int32, (nb, HEAD_DIM), 0)
    row = lax.broadcasted_iota(jnp.int32, (SUBLANES, HEAD_DIM), 0)
    out = jnp.zeros((SUBLANES, HEAD_DIM), jnp.int32)
    for r in range(MOBA_TOPK):
        best = jnp.max(gate, axis=0, keepdims=True)
        idx = jnp.min(jnp.where(gate == best, blk, nb), axis=0, keepdims=True)
        out = jnp.where(row == r, idx, out)
        gate = jnp.where(blk == idx, -jnp.inf, gate)
    o_ref[...] = out


def _sample_choose(q, kmean_l):
    n = q.shape[0]
    parts, bps = kmean_l.shape[1], kmean_l.shape[3]
    return pl.pallas_call(
        functools.partial(_choose_kernel, nb=parts * bps),
        grid=(n,),
        in_specs=[pl.BlockSpec((None, N_HEADS, HEAD_DIM), lambda b: (b, 0, 0)),
                  pl.BlockSpec((None, parts, N_KV_HEADS, bps, HEAD_DIM), lambda b: (b, 0, 0, 0, 0))],
        out_specs=pl.BlockSpec((None, SUBLANES, HEAD_DIM), lambda b: (b, 0, 0)),
        out_shape=jax.ShapeDtypeStruct((n, SUBLANES, HEAD_DIM), jnp.int32),
        name="sample_choose",
        compiler_params=pltpu.CompilerParams(dimension_semantics=("arbitrary",)),
    )(q.reshape(n, N_HEADS, HEAD_DIM), kmean_l)


def _sattn_kernel(sel_ref, pt_ref, q_ref, kn_ref, vn_ref, ck_ref, cv_ref, o_ref, kbuf, vbuf, sem,
                  *, l, n_pages):
    b = pl.program_id(0)
    n = pl.num_programs(0)
    slot = b % 2

    def copies(seq, sl):
        out = []
        for h in range(N_HEADS):
            kv = h // GROUP
            for r in range(MOBA_TOPK):
                blk = sel_ref[(seq * MOBA_TOPK + r) * N_HEADS + h]
                for p in range(PAGES_PER_BLOCK):
                    page = pt_ref[seq * n_pages + blk * PAGES_PER_BLOCK + p]
                    part = r * PAGES_PER_BLOCK + p
                    for c, (src, dst) in enumerate(((ck_ref, kbuf), (cv_ref, vbuf))):
                        out.append(pltpu.make_async_copy(
                            src.at[l, page, :, kv, :],
                            dst.at[sl, h, pl.ds(part * PAGE_SIZE, PAGE_SIZE), :],
                            sem.at[sl, c, h, part]))
        return out

    @pl.when(b == 0)
    def _():
        for cp in copies(b, 0):
            cp.start()

    for cp in copies(b, slot):
        cp.wait()
    for cp in copies(jnp.minimum(b + 1, n - 1), 1 - slot):
        cp.start()

    for h in range(N_HEADS):
        kv = h // GROUP
        q = q_ref[h:h + 1, :]
        s = jnp.sum(kbuf[slot, h] * q, axis=1, keepdims=True) * ATT_SCALE
        s_new = jnp.sum(kn_ref[kv:kv + 1, :] * q, axis=1, keepdims=True) * ATT_SCALE
        m = jnp.maximum(jnp.max(s, axis=0, keepdims=True), s_new)
        p = jnp.exp(s - m)
        p_new = jnp.exp(s_new - m)
        den = jnp.sum(p, axis=0, keepdims=True) + p_new
        num = jnp.sum(p * vbuf[slot, h], axis=0, keepdims=True) + p_new * vn_ref[kv:kv + 1, :]
        o_ref[h:h + 1, :] = num / den

    @pl.when(b == n - 1)
    def _():
        for cp in copies(b, 1 - slot):
            cp.wait()


def _sample_attn(l, sel, page_table, q, k_new, v_new, cache_k, cache_v):
    n = q.shape[0]
    n_pages = page_table.shape[1]
    rows = MOBA_TOPK * MOBA_BLOCK
    grid_spec = pltpu.PrefetchScalarGridSpec(
        num_scalar_prefetch=2,
        grid=(n,),
        in_specs=[pl.BlockSpec((None, N_HEADS, HEAD_DIM), lambda b, *_: (b, 0, 0)),
                  pl.BlockSpec((None, N_KV_HEADS, HEAD_DIM), lambda b, *_: (b, 0, 0)),
                  pl.BlockSpec((None, N_KV_HEADS, HEAD_DIM), lambda b, *_: (b, 0, 0)),
                  pl.BlockSpec(memory_space=pl.ANY),
                  pl.BlockSpec(memory_space=pl.ANY)],
        out_specs=pl.BlockSpec((None, N_HEADS, HEAD_DIM), lambda b, *_: (b, 0, 0)),
        scratch_shapes=[pltpu.VMEM((2, N_HEADS, rows, HEAD_DIM), F32),
                        pltpu.VMEM((2, N_HEADS, rows, HEAD_DIM), F32),
                        pltpu.SemaphoreType.DMA((2, 2, N_HEADS, MOBA_TOPK * PAGES_PER_BLOCK))],
    )
    sel_flat = sel[:, :MOBA_TOPK, :N_HEADS].reshape(-1)
    out = pl.pallas_call(
        functools.partial(_sattn_kernel, l=l, n_pages=n_pages),
        grid_spec=grid_spec,
        out_shape=jax.ShapeDtypeStruct((n, N_HEADS, HEAD_DIM), F32),
        name=f"sample_attn_{l}",
        compiler_params=pltpu.CompilerParams(dimension_semantics=("arbitrary",)),
    )(sel_flat, page_table.reshape(-1), q.reshape(n, N_HEADS, HEAD_DIM),
      k_new.reshape(n, N_KV_HEADS, HEAD_DIM), v_new.reshape(n, N_KV_HEADS, HEAD_DIM), cache_k, cache_v)
    return out.reshape(n, N_HEADS * HEAD_DIM)


def _trunk(x_prompt, x_sample, cache_k, cache_v, state_sconv, state_rconv, state_lru, state_fconv,
           page_table, w, tm):
    depth = w['w_in'].shape[0]
    bsz, t, d = x_prompt.shape
    n = x_sample.shape[0]
    xp = x_prompt
    xs = x_sample.reshape(n, d)
    sst = jnp.swapaxes(state_sconv, 1, 2)
    rst = jnp.swapaxes(state_rconv, 1, 2)
    fst = jnp.swapaxes(state_fconv, 1, 2)

    st_p, st_s = [], []
    for l in range(depth):
        final = l == depth - 1
        q, kb, vb, kf, vf, km, g0, mp, scn, rcn, hl = _prompt_mix(l, xp, w, tm)
        att = _prompt_attn(q, kb, vb, km.reshape(bsz, t // MOBA_BLOCK, KV_W))
        xp, fcn, kmean_s = _prompt_post(l, xp, att, g0, mp, w, cache_k, page_table, tm, final)
        st_p.append((kf.reshape(bsz, t, N_KV_HEADS, HEAD_DIM), vf.reshape(bsz, t, N_KV_HEADS, HEAD_DIM),
                     scn, rcn, hl.reshape(bsz, d), fcn))

        qs, ks, vs, g0s, mps, scns, rcns, hls = _sample_mix(l, xs, sst, rst, state_lru, w)
        sel = _sample_choose(qs, kmean_s)
        atts = _sample_attn(l, sel, page_table, qs, ks, vs, cache_k, cache_v)
        xs, fcns = _sample_post(l, xs, atts, g0s, mps, fst, w, final)
        st_s.append((ks.reshape(n, 1, N_KV_HEADS, HEAD_DIM), vs.reshape(n, 1, N_KV_HEADS, HEAD_DIM),
                     jnp.swapaxes(scns, 0, 1), jnp.swapaxes(rcns, 0, 1), hls, jnp.swapaxes(fcns, 0, 1)))

    outs_p = [jnp.stack([s[i] for s in st_p], axis=0) for i in range(6)]
    outs_s = [jnp.stack([s[i] for s in st_s], axis=0) for i in range(6)]
    return (xp, xs.reshape(n, 1, d), *outs_p, *outs_s)


def kernel(x_prompt, x_sample, cache_k, cache_v, state_sconv, state_rconv, state_lru, state_fconv, page_table, norm1, w_in, sconv_w, rconv_w, rconv_b, lru_wa, lru_ba, lru_wx, lru_bx, lru_lambda, w_branch, w_merge, b_merge, w_o, norm2, w_up, fconv_w, fconv_b, w_down, norm_f):
    row = lambda a: a.reshape(a.shape[0], 1, a.shape[1])
    w = dict(
        norm1=row(norm1), w_in=w_in.astype(BF16), sconv_w=sconv_w, rconv_w=rconv_w, rconv_b=row(rconv_b),
        lru_wa=lru_wa.astype(BF16), lru_ba=row(lru_ba), lru_wx=lru_wx.astype(BF16), lru_bx=row(lru_bx),
        lru_wax=jnp.concatenate([lru_wa, lru_wx], axis=-1).astype(BF16),
        lru_lambda=row(lru_lambda), w_branch=w_branch.astype(BF16), w_merge=w_merge.astype(BF16),
        b_merge=row(b_merge), w_o=w_o.astype(BF16), norm2=row(norm2), w_up=w_up.astype(BF16),
        fconv_w=fconv_w, fconv_b=row(fconv_b), w_down=w_down.astype(BF16), norm_f=norm_f.reshape(1, -1),
    )
    return _trunk(x_prompt, x_sample, cache_k, cache_v, state_sconv, state_rconv, state_lru, state_fconv,
                  page_table, w, TIME_TILE)
```

```python
import functools
import math

import jax
import jax.numpy as jnp
from jax import lax
from jax.experimental import pallas as pl
from jax.experimental.pallas import tpu as pltpu

F32 = jnp.float32
BF16 = jnp.bfloat16

HEAD_DIM = 128
N_HEADS = 8
N_KV_HEADS = 4
GROUP = N_HEADS // N_KV_HEADS
KV_W = N_KV_HEADS * HEAD_DIM
MOBA_BLOCK = 256
MOBA_TOPK = 3
LRU_BLOCKS = 8
LRU_C = 8.0
RMS_EPS = 1e-6
PAGE_SIZE = 128
PAGES_PER_BLOCK = MOBA_BLOCK // PAGE_SIZE
ATT_SCALE = HEAD_DIM ** -0.5
LOG2_E = math.log2(math.e)

SUBLANES = 8
HALO = SUBLANES
VMEM_LIMIT = 56 * 1024 * 1024
POST_VMEM_LIMIT = 60 * 1024 * 1024
TIME_TILE = 256


def _dot(a, b):
    return jnp.dot(a, b, preferred_element_type=F32)


def _dot_nt(a, b, precision=None):
    return lax.dot_general(a, b, (((1,), (1,)), ((), ())), preferred_element_type=F32,
                           precision=precision)


def _dot_tn(a, b):
    return lax.dot_general(a, b, (((0,), (0,)), ((), ())), preferred_element_type=F32)


def _rms(x, g):
    return x * lax.rsqrt(jnp.mean(x * x, axis=-1, keepdims=True) + RMS_EPS) * g


def _gelu_tanh(x):
    c = math.sqrt(2.0 / math.pi)
    return x * (0.5 * (1.0 + jnp.tanh(c * (x + 0.044715 * (x * x * x)))))


def _softplus(x):
    return jnp.maximum(x, 0.0) + jnp.log1p(jnp.exp(-jnp.abs(x)))


def _lru_coeffs(xc, r_lin, i_lin, lam):
    r = jax.nn.sigmoid(r_lin)
    i = jax.nn.sigmoid(i_lin)
    log_a = -LRU_C * r * _softplus(-lam)
    a = jnp.exp(log_a)
    th = jnp.tanh(log_a)
    b = jnp.sqrt(-2.0 * th / (1.0 - th)) * (i * xc)
    return a, b


def _const_spec(shape, index):
    return pl.BlockSpec(shape, lambda *_: index, pipeline_mode=pl.Buffered(1))


def _mix_kernel(x_ref, g1_ref, win_ref, scw_ref, rcw_ref, rcb_ref, wax_ref, ba_ref, bx_ref,
                lam_ref, wb1_ref, wb2_ref, wm_ref, bm_ref, kall_ref, vall_ref,
                q_ref, kb_ref, vb_ref, kf_ref, vf_ref, km_ref, g0_ref, mp_ref, scn_ref, rcn_ref, hl_ref,
                ext_sc, ext_rc, lin_a, lin_i, h_carry, *, tm, d):
    del kall_ref, vall_ref
    t = pl.program_id(1)
    last = pl.num_programs(1) - 1

    @pl.when(t == 0)
    def _():
        ext_sc[0:HALO, :] = jnp.zeros((HALO, d), F32)
        ext_rc[0:HALO, :] = jnp.zeros((HALO, d), F32)
        h_carry[...] = jnp.zeros((1, d), F32)

    xn = _rms(x_ref[...], g1_ref[...]).astype(BF16)
    c0 = d + 2 * KV_W
    bw = d // LRU_BLOCKS

    ext_rc[HALO:HALO + tm, :] = _dot(xn, win_ref[:, c0 + 3 * d:c0 + 4 * d])
    u = _dot(xn, win_ref[:, c0 + d:c0 + 2 * d]) * _dot(xn, win_ref[:, c0 + 2 * d:c0 + 3 * d])
    xc = (rcw_ref[0:1, :] * ext_rc[HALO - 3:HALO - 3 + tm, :]
          + rcw_ref[1:2, :] * ext_rc[HALO - 2:HALO - 2 + tm, :]
          + rcw_ref[2:3, :] * ext_rc[HALO - 1:HALO - 1 + tm, :]
          + rcw_ref[3:4, :] * ext_rc[HALO:HALO + tm, :]) + rcb_ref[...]
    xcb = xc.astype(BF16)
    for n in range(LRU_BLOCKS):
        both = _dot(xcb[:, n * bw:(n + 1) * bw], wax_ref[n])
        lin_a[:, n * bw:(n + 1) * bw] = both[:, 0:bw]
        lin_i[:, n * bw:(n + 1) * bw] = both[:, bw:2 * bw]

    b_gate = _dot(xn, win_ref[:, c0:c0 + d])
    q_ref[...] = _dot(xn, win_ref[:, 0:d]).astype(BF16)
    ext_sc[HALO:HALO + tm, :] = u
    sc = (scw_ref[0:1, :] * ext_sc[HALO - 2:HALO - 2 + tm, :]
          + scw_ref[1:2, :] * ext_sc[HALO - 1:HALO - 1 + tm, :]
          + scw_ref[2:3, :] * u)
    o_sc = (b_gate * sc).astype(BF16)

    a, b = _lru_coeffs(xc, lin_a[...] + ba_ref[...], lin_i[...] + bx_ref[...], lam_ref[...])
    lin_a[...] = a
    lin_i[...] = b

    k = _dot(xn, win_ref[:, d:d + KV_W])
    v = _dot(xn, win_ref[:, d + KV_W:d + 2 * KV_W])
    kb_ref[...] = k.astype(BF16)
    vb_ref[...] = v.astype(BF16)
    for h in range(N_KV_HEADS):
        kf_ref[pl.ds(h, tm, stride=N_KV_HEADS), :] = k[:, h * HEAD_DIM:(h + 1) * HEAD_DIM]
        vf_ref[pl.ds(h, tm, stride=N_KV_HEADS), :] = v[:, h * HEAD_DIM:(h + 1) * HEAD_DIM]
    for j in range(tm // MOBA_BLOCK):
        km_ref[j:j + 1, :] = jnp.mean(k[j * MOBA_BLOCK:(j + 1) * MOBA_BLOCK], axis=0, keepdims=True)
    g0_ref[...] = jax.nn.sigmoid(_dot(xn, wm_ref[:, 0:d]) + bm_ref[:, 0:d])

    row = lax.broadcasted_iota(jnp.int32, (SUBLANES, d), 0)
    h_prev = jnp.broadcast_to(h_carry[...], (SUBLANES, d))
    for g in range(tm // SUBLANES):
        rows = slice(g * SUBLANES, (g + 1) * SUBLANES)
        ag = lin_a[rows, :]
        bg = lin_i[rows, :]
        s = 1
        while s < SUBLANES:
            keep = row >= s
            bg = bg + ag * jnp.where(keep, pltpu.roll(bg, s, 0), 0.0)
            ag = ag * jnp.where(keep, pltpu.roll(ag, s, 0), 1.0)
            s *= 2
        hg = ag * h_prev + bg
        lin_i[rows, :] = hg
        h_prev = jnp.broadcast_to(hg[SUBLANES - 1:SUBLANES, :], (SUBLANES, d))
    h_carry[...] = h_prev[0:1, :]

    pb1 = _dot(o_sc, wb1_ref[...])
    g_lru = _dot(xn, win_ref[:, c0 + 4 * d:c0 + 5 * d])
    mp = jax.nn.sigmoid(_dot(xn, wm_ref[:, d:2 * d]) + bm_ref[:, d:2 * d]) * pb1
    gate2 = jax.nn.sigmoid(_dot(xn, wm_ref[:, 2 * d:3 * d]) + bm_ref[:, 2 * d:3 * d])
    o_lru = (lin_i[...] * _gelu_tanh(g_lru)).astype(BF16)
    mp_ref[...] = mp + gate2 * _dot(o_lru, wb2_ref[...])

    @pl.when(t == last)
    def _():
        scn_ref[...] = ext_sc[HALO + tm - 2:HALO + tm, :]
        rcn_ref[...] = ext_rc[HALO + tm - 3:HALO + tm, :]
        hl_ref[...] = h_carry[...]

    ext_sc[HALO - 2:HALO, :] = ext_sc[HALO + tm - 2:HALO + tm, :]
    ext_rc[HALO - 3:HALO, :] = ext_rc[HALO + tm - 3:HALO + tm, :]


def _prompt_mix(l, x, w, k_all, v_all, tm):
    bsz, t, d = x.shape
    nt = t // tm
    nmb = tm // MOBA_BLOCK
    n_in = w['w_in'].shape[-1]
    row = lambda b, i: (b, i, 0)
    per_b = lambda b, i: (b, 0, 0)
    vec = lambda n: _const_spec((None, 1, n), (l, 0, 0))
    in_specs = [
        pl.BlockSpec((None, tm, d), row),
        vec(d),
        _const_spec((None, d, n_in), (l, 0, 0)),
        _const_spec((None, 3, d), (l, 0, 0)),
        _const_spec((None, 4, d), (l, 0, 0)),
        vec(d),
        _const_spec((None, LRU_BLOCKS, d // LRU_BLOCKS, 2 * d // LRU_BLOCKS), (l, 0, 0, 0)),
        vec(d),
        vec(d),
        vec(d),
        _const_spec((None, None, d, d), (l, 1, 0, 0)),
        _const_spec((None, None, d, d), (l, 2, 0, 0)),
        _const_spec((None, d, 3 * d), (l, 0, 0)),
        vec(3 * d),
        pl.BlockSpec(memory_space=pl.ANY),
        pl.BlockSpec(memory_space=pl.ANY),
    ]
    out_shape = [
        jax.ShapeDtypeStruct((bsz, t, d), BF16),
        jax.ShapeDtypeStruct((bsz, t, KV_W), BF16),
        jax.ShapeDtypeStruct((bsz, t, KV_W), BF16),
        jax.ShapeDtypeStruct(k_all.shape, F32),
        jax.ShapeDtypeStruct(v_all.shape, F32),
        jax.ShapeDtypeStruct((bsz, nt, nmb, KV_W), F32),
        jax.ShapeDtypeStruct((bsz, t, d), F32),
        jax.ShapeDtypeStruct((bsz, t, d), F32),
        jax.ShapeDtypeStruct((bsz, 2, d), F32),
        jax.ShapeDtypeStruct((bsz, 3, d), F32),
        jax.ShapeDtypeStruct((bsz, 1, d), F32),
    ]
    out_specs = [
        pl.BlockSpec((None, tm, d), row),
        pl.BlockSpec((None, tm, KV_W), row),
        pl.BlockSpec((None, tm, KV_W), row),
        pl.BlockSpec((None, None, tm * N_KV_HEADS, HEAD_DIM), lambda b, i: (l, b, i, 0)),
        pl.BlockSpec((None, None, tm * N_KV_HEADS, HEAD_DIM), lambda b, i: (l, b, i, 0)),
        pl.BlockSpec((None, None, nmb, KV_W), lambda b, i: (b, i, 0, 0)),
        pl.BlockSpec((None, tm, d), row),
        pl.BlockSpec((None, tm, d), row),
        pl.BlockSpec((None, 2, d), per_b),
        pl.BlockSpec((None, 3, d), per_b),
        pl.BlockSpec((None, 1, d), per_b),
    ]
    scratch = [
        pltpu.VMEM((HALO + tm, d), F32),
        pltpu.VMEM((HALO + tm, d), F32),
        pltpu.VMEM((tm, d), F32),
        pltpu.VMEM((tm, d), F32),
        pltpu.VMEM((1, d), F32),
    ]
    return pl.pallas_call(
        functools.partial(_mix_kernel, tm=tm, d=d),
        grid=(bsz, nt), in_specs=in_specs, out_specs=out_specs, out_shape=out_shape,
        scratch_shapes=scratch, name=f"prompt_mix_{l}",
        input_output_aliases={len(in_specs) - 2: 3, len(in_specs) - 1: 4},
        compiler_params=pltpu.CompilerParams(
            dimension_semantics=("arbitrary", "arbitrary"), vmem_limit_bytes=VMEM_LIMIT),
    )(x, w['norm1'], w['w_in'], w['sconv_w'], w['rconv_w'], w['rconv_b'], w['lru_wax'], w['lru_ba'],
      w['lru_bx'], w['lru_lambda'], w['w_branch'], w['w_branch'], w['w_merge'], w['b_merge'], k_all, v_all)


def _attn_kernel(q_ref, k_ref, v_ref, km_ref, o_ref, *, nb):
    blk = MOBA_BLOCK
    nq = GROUP * blk

    parts = []
    rest = km_ref[...]
    for _ in range(3):
        part = rest.astype(BF16).astype(F32)
        parts += [part, jnp.zeros_like(part)]
        rest = rest - part
    km_terms = jnp.concatenate(parts, axis=0).astype(BF16)
    prow = 2 * nb

    block_id = lax.broadcasted_iota(jnp.int32, (nb, nq), 0)
    key = lax.broadcasted_iota(jnp.int32, (blk, nq), 0)
    qpos = lax.broadcasted_iota(jnp.int32, (blk, nq), 1) & (blk - 1)
    causal = key <= qpos

    def scores(i):
        q2 = jnp.concatenate([q_ref[i * blk:(i + 1) * blk, g * HEAD_DIM:(g + 1) * HEAD_DIM]
                              for g in range(GROUP)], axis=0)
        keys = k_ref[0:(i + 1) * blk, :]
        return _dot_nt(keys if i == 0 else jnp.concatenate([keys, km_terms], axis=0), q2)

    s_next = scores(0)
    for i in range(nb):
        q_rows = slice(i * blk, (i + 1) * blk)
        n_keys = (i + 1) * blk
        s_all = s_next
        if i + 1 < nb:
            s_next = scores(i + 1)
        if i > 0:
            gate = (s_all[n_keys + 2 * prow:n_keys + 2 * prow + nb]
                    + s_all[n_keys + prow:n_keys + prow + nb] + s_all[n_keys:n_keys + nb])
            past = block_id < i
            gate = jnp.where(past, gate, -jnp.inf)
            rank = jnp.zeros((nb, nq), jnp.int32)
            for j in range(i):
                gj = gate[j:j + 1, :]
                rank = rank + jnp.where((gj > gate) | ((gj == gate) & (j < block_id)), 1, 0)
            sel = jnp.where(past & (rank < MOBA_TOPK), 1.0, 0.0)

        pieces = [jnp.where(sel[j:j + 1, :] > 0.5, s_all[j * blk:(j + 1) * blk], -jnp.inf) for j in range(i)]
        pieces.append(jnp.where(causal, s_all[i * blk:n_keys], -jnp.inf))
        m = jnp.max(pieces[-1], axis=0, keepdims=True)
        for piece in pieces[:-1]:
            m = jnp.maximum(m, jnp.max(piece, axis=0, keepdims=True))
        probs = [jnp.exp2((piece - m) * (ATT_SCALE * LOG2_E)) for piece in pieces]
        den = jnp.sum(probs[0], axis=0, keepdims=True)
        for pr in probs[1:]:
            den = den + jnp.sum(pr, axis=0, keepdims=True)
        p_all = jnp.concatenate([pr.astype(BF16) for pr in probs], axis=0)
        out = (_dot_tn(v_ref[0:n_keys, :], p_all) / den).T
        for g in range(GROUP):
            o_ref[q_rows, g * HEAD_DIM:(g + 1) * HEAD_DIM] = out[g * blk:(g + 1) * blk].astype(o_ref.dtype)


def _prompt_attn(q, k, v, kmean):
    bsz, t, d = q.shape
    nb = t // MOBA_BLOCK
    gw = GROUP * HEAD_DIM
    return pl.pallas_call(
        functools.partial(_attn_kernel, nb=nb),
        grid=(bsz, N_KV_HEADS),
        in_specs=[
            pl.BlockSpec((None, t, gw), lambda b, h: (b, 0, h)),
            pl.BlockSpec((None, t, HEAD_DIM), lambda b, h: (b, 0, h)),
            pl.BlockSpec((None, t, HEAD_DIM), lambda b, h: (b, 0, h)),
            pl.BlockSpec((None, nb, HEAD_DIM), lambda b, h: (b, 0, h)),
        ],
        out_specs=pl.BlockSpec((None, t, gw), lambda b, h: (b, 0, h)),
        out_shape=jax.ShapeDtypeStruct((bsz, t, d), BF16),
        name="prompt_attn",
        compiler_params=pltpu.CompilerParams(
            dimension_semantics=("arbitrary", "arbitrary"), vmem_limit_bytes=VMEM_LIMIT),
    )(q, k, v, kmean)


def _post_kernel(pt_ref, x_ref, att_ref, g0_ref, mp_ref, wb0_ref, wo_ref, g2_ref, wup_ref, fcw_ref, fcb_ref,
                 wdn_ref, gf_ref, cache_ref, y_ref, fcn_ref, km_ref, ext_fc, cbuf, csem,
                 *, layer, tm, d, dff, final, pps):
    t = pl.program_id(1)
    last = pl.num_programs(1) - 1
    step = pl.program_id(0) * pl.num_programs(1) + t
    n_steps = pl.num_programs(0) * pl.num_programs(1)
    slot = step % 2

    def page_copies(s, sl):
        return [pltpu.make_async_copy(cache_ref.at[layer, pt_ref[s * pps + p]], cbuf.at[sl, p], csem.at[sl, p])
                for p in range(pps)]

    @pl.when(step == 0)
    def _():
        for cp in page_copies(step, 0):
            cp.start()

    for cp in page_copies(step, slot):
        cp.wait()

    merged = g0_ref[...] * _dot(att_ref[...], wb0_ref[...]) + mp_ref[...]
    x1 = x_ref[...] + _dot(merged.astype(BF16), wo_ref[...])

    for cp in page_copies(jnp.minimum(step + 1, n_steps - 1), 1 - slot):
        cp.start()
    groups = PAGE_SIZE * N_KV_HEADS // SUBLANES
    for j in range(pps // PAGES_PER_BLOCK):
        tot = jnp.zeros((SUBLANES, HEAD_DIM), F32)
        for p in range(PAGES_PER_BLOCK):
            page = cbuf.at[slot, PAGES_PER_BLOCK * j + p]
            for g in range(groups):
                tot = tot + page[g * SUBLANES:(g + 1) * SUBLANES, :]
        tot = (tot[0:N_KV_HEADS] + tot[N_KV_HEADS:2 * N_KV_HEADS]) * (1.0 / MOBA_BLOCK)
        for h in range(N_KV_HEADS):
            km_ref[h, j:j + 1, :] = tot[h:h + 1, :]

    @pl.when(t == 0)
    def _():
        ext_fc[0:HALO, :] = jnp.zeros((HALO, 2 * dff), F32)

    xn2 = _rms(x1, g2_ref[...]).astype(BF16)
    ext_fc[HALO:HALO + tm, :] = _dot(xn2, wup_ref[...])

    def conv(c0, c1):
        return (fcw_ref[0:1, c0:c1] * ext_fc[HALO - 2:HALO - 2 + tm, c0:c1]
                + fcw_ref[1:2, c0:c1] * ext_fc[HALO - 1:HALO - 1 + tm, c0:c1]
                + fcw_ref[2:3, c0:c1] * ext_fc[HALO:HALO + tm, c0:c1]) + fcb_ref[:, c0:c1]

    hh = (jax.nn.silu(conv(0, dff)) * conv(dff, 2 * dff)).astype(BF16)
    x2 = x1 + _dot(hh, wdn_ref[...])
    y_ref[...] = _rms(x2, gf_ref[...]) if final else x2

    @pl.when(t == last)
    def _():
        fcn_ref[...] = ext_fc[HALO + tm - 2:HALO + tm, :]

    ext_fc[HALO - 2:HALO, :] = ext_fc[HALO + tm - 2:HALO + tm, :]

    @pl.when(step == n_steps - 1)
    def _():
        for cp in page_copies(step, 1 - slot):
            cp.wait()


def _prompt_post(l, x, att, g0, mp, w, cache_k, page_table, tm, final):
    bsz, t, d = x.shape
    dff = w['w_down'].shape[1]
    depth, n_pool, page, hkv, hd = cache_k.shape
    n_b, n_pages = page_table.shape
    n_steps = bsz * (t // tm)
    pps = n_b * n_pages // n_steps
    assert pps * n_steps == n_b * n_pages and n_pages % pps == 0
    assert pps % (PAGES_PER_BLOCK * SUBLANES) == 0
    bps = pps // PAGES_PER_BLOCK
    nt = t // tm
    row = lambda b, i, pt: (b, i, 0)
    tile = pl.BlockSpec((None, tm, d), row)
    in_specs = [
        tile, tile, tile, tile,
        _const_spec((None, None, d, d), (l, 0, 0, 0)),
        _const_spec((None, d, d), (l, 0, 0)),
        _const_spec((None, 1, d), (l, 0, 0)),
        _const_spec((None, d, 2 * dff), (l, 0, 0)),
        _const_spec((None, 3, 2 * dff), (l, 0, 0)),
        _const_spec((None, 1, 2 * dff), (l, 0, 0)),
        _const_spec((None, dff, d), (l, 0, 0)),
        _const_spec((1, d), (0, 0)),
        pl.BlockSpec(memory_space=pl.ANY),
    ]
    grid_spec = pltpu.PrefetchScalarGridSpec(
        num_scalar_prefetch=1,
        grid=(bsz, nt),
        in_specs=in_specs,
        out_specs=[tile,
                   pl.BlockSpec((None, 2, 2 * dff), lambda b, i, pt: (b, 0, 0)),
                   pl.BlockSpec((None, hkv, bps, hd), lambda b, i, pt: (b * nt + i, 0, 0, 0))],
        scratch_shapes=[pltpu.VMEM((HALO + tm, 2 * dff), F32),
                        pltpu.VMEM((2, pps, page * hkv, hd), F32),
                        pltpu.SemaphoreType.DMA((2, pps))],
    )
    y, fcn, km = pl.pallas_call(
        functools.partial(_post_kernel, layer=l, tm=tm, d=d, dff=dff, final=final, pps=pps),
        grid_spec=grid_spec,
        out_shape=[jax.ShapeDtypeStruct((bsz, t, d), F32),
                   jax.ShapeDtypeStruct((bsz, 2, 2 * dff), F32),
                   jax.ShapeDtypeStruct((n_steps, hkv, bps, hd), F32)],
        name=f"prompt_post_{l}",
        compiler_params=pltpu.CompilerParams(
            dimension_semantics=("arbitrary", "arbitrary"), vmem_limit_bytes=POST_VMEM_LIMIT),
    )(page_table.reshape(-1), x, att, g0, mp, w['w_branch'], w['w_o'], w['norm2'], w['w_up'], w['fconv_w'],
      w['fconv_b'], w['w_down'], w['norm_f'], cache_k.reshape(depth, n_pool, page * hkv, hd))
    return y, fcn, km.reshape(n_b, n_pages // pps, hkv, bps, hd)


def _smix_kernel(x_ref, g1_ref, win_ref, scw_ref, rcw_ref, rcb_ref, wa_ref, ba_ref, wx_ref, bx_ref,
                 lam_ref, wb1_ref, wb2_ref, wm_ref, bm_ref, sst_ref, rst_ref, h0_ref,
                 q_ref, k_ref, v_ref, g0_ref, mp_ref, scn_ref, rcn_ref, hl_ref, *, d):
    xn = _rms(x_ref[...], g1_ref[...]).astype(BF16)
    q_ref[...] = _dot(xn, win_ref[:, 0:d])
    k_ref[...] = _dot(xn, win_ref[:, d:d + KV_W])
    v_ref[...] = _dot(xn, win_ref[:, d + KV_W:d + 2 * KV_W])
    c0 = d + 2 * KV_W

    u = _dot(xn, win_ref[:, c0 + d:c0 + 2 * d]) * _dot(xn, win_ref[:, c0 + 2 * d:c0 + 3 * d])
    sc = scw_ref[0:1, :] * sst_ref[0] + scw_ref[1:2, :] * sst_ref[1] + scw_ref[2:3, :] * u
    scn_ref[0] = sst_ref[1]
    scn_ref[1] = u
    o_sc = (_dot(xn, win_ref[:, c0:c0 + d]) * sc).astype(BF16)
    pb1 = _dot(o_sc, wb1_ref[...])

    x_lru = _dot(xn, win_ref[:, c0 + 3 * d:c0 + 4 * d])
    xc = (rcw_ref[0:1, :] * rst_ref[0] + rcw_ref[1:2, :] * rst_ref[1] + rcw_ref[2:3, :] * rst_ref[2]
          + rcw_ref[3:4, :] * x_lru) + rcb_ref[...]
    rcn_ref[0] = rst_ref[1]
    rcn_ref[1] = rst_ref[2]
    rcn_ref[2] = x_lru
    bw = d // LRU_BLOCKS
    xcb = xc.astype(BF16)
    r_lin = jnp.concatenate(
        [_dot(xcb[:, n * bw:(n + 1) * bw], wa_ref[n]) for n in range(LRU_BLOCKS)], axis=1)
    i_lin = jnp.concatenate(
        [_dot(xcb[:, n * bw:(n + 1) * bw], wx_ref[n]) for n in range(LRU_BLOCKS)], axis=1)
    a, b = _lru_coeffs(xc, r_lin + ba_ref[...], i_lin + bx_ref[...], lam_ref[...])
    h = a * h0_ref[...] + b
    hl_ref[...] = h
    o_lru = (h * _gelu_tanh(_dot(xn, win_ref[:, c0 + 4 * d:c0 + 5 * d]))).astype(BF16)
    pb2 = _dot(o_lru, wb2_ref[...])

    g0_ref[...] = jax.nn.sigmoid(_dot(xn, wm_ref[:, 0:d]) + bm_ref[:, 0:d])
    mp_ref[...] = (jax.nn.sigmoid(_dot(xn, wm_ref[:, d:2 * d]) + bm_ref[:, d:2 * d]) * pb1
                   + jax.nn.sigmoid(_dot(xn, wm_ref[:, 2 * d:3 * d]) + bm_ref[:, 2 * d:3 * d]) * pb2)


def _sample_mix(l, x, sst, rst, h0, w):
    n, d = x.shape
    n_in = w['w_in'].shape[-1]
    full = lambda shape: pl.BlockSpec(shape, lambda i: (0,) * len(shape))
    lay = lambda shape, idx: pl.BlockSpec(shape, lambda i: idx)
    vec = lambda m: lay((None, 1, m), (l, 0, 0))
    lw = (None, LRU_BLOCKS, d // LRU_BLOCKS, d // LRU_BLOCKS)
    in_specs = [
        full((n, d)), vec(d), lay((None, d, n_in), (l, 0, 0)), lay((None, 3, d), (l, 0, 0)),
        lay((None, 4, d), (l, 0, 0)), vec(d), lay(lw, (l, 0, 0, 0)), vec(d), lay(lw, (l, 0, 0, 0)),
        vec(d), vec(d), lay((None, None, d, d), (l, 1, 0, 0)), lay((None, None, d, d), (l, 2, 0, 0)),
        lay((None, d, 3 * d), (l, 0, 0)), vec(3 * d),
        lay((None, 2, n, d), (l, 0, 0, 0)), lay((None, 3, n, d), (l, 0, 0, 0)), lay((None, n, d), (l, 0, 0)),
    ]
    out_shape = [
        jax.ShapeDtypeStruct((n, d), F32), jax.ShapeDtypeStruct((n, KV_W), F32),
        jax.ShapeDtypeStruct((n, KV_W), F32), jax.ShapeDtypeStruct((n, d), F32),
        jax.ShapeDtypeStruct((n, d), F32), jax.ShapeDtypeStruct((2, n, d), F32),
        jax.ShapeDtypeStruct((3, n, d), F32), jax.ShapeDtypeStruct((n, d), F32),
    ]
    out_specs = [full(s.shape) for s in out_shape]
    return pl.pallas_call(
        functools.partial(_smix_kernel, d=d),
        grid=(1,), in_specs=in_specs, out_specs=out_specs, out_shape=out_shape,
        name=f"sample_mix_{l}",
        compiler_params=pltpu.CompilerParams(
            dimension_semantics=("arbitrary",), vmem_limit_bytes=VMEM_LIMIT),
    )(x, w['norm1'], w['w_in'], w['sconv_w'], w['rconv_w'], w['rconv_b'], w['lru_wa'], w['lru_ba'],
      w['lru_wx'], w['lru_bx'], w['lru_lambda'], w['w_branch'], w['w_branch'], w['w_merge'],
      w['b_merge'], sst, rst, h0)


def _spost_kernel(x_ref, att_ref, g0_ref, mp_ref, wb0_ref, wo_ref, g2_ref, wup_ref, fcw_ref, fcb_ref,
                  wdn_ref, gf_ref, fst_ref, y_ref, fcn_ref, *, dff, final):
    merged = g0_ref[...] * _dot(att_ref[...].astype(BF16), wb0_ref[...]) + mp_ref[...]
    x1 = x_ref[...] + _dot(merged.astype(BF16), wo_ref[...])
    xn2 = _rms(x1, g2_ref[...]).astype(BF16)
    up = _dot(xn2, wup_ref[...])
    c = (fcw_ref[0:1, :] * fst_ref[0] + fcw_ref[1:2, :] * fst_ref[1] + fcw_ref[2:3, :] * up) + fcb_ref[...]
    fcn_ref[0] = fst_ref[1]
    fcn_ref[1] = up
    hh = (jax.nn.silu(c[:, 0:dff]) * c[:, dff:2 * dff]).astype(BF16)
    x2 = x1 + _dot(hh, wdn_ref[...])
    y_ref[...] = _rms(x2, gf_ref[...]) if final else x2


def _sample_post(l, x, att, g0, mp, fst, w, final):
    n, d = x.shape
    dff = w['w_down'].shape[1]
    full = lambda shape: pl.BlockSpec(shape, lambda i: (0,) * len(shape))
    lay = lambda shape, idx: pl.BlockSpec(shape, lambda i: idx)
    in_specs = [
        full((n, d)), full((n, d)), full((n, d)), full((n, d)),
        lay((None, None, d, d), (l, 0, 0, 0)), lay((None, d, d), (l, 0, 0)), lay((None, 1, d), (l, 0, 0)),
        lay((None, d, 2 * dff), (l, 0, 0)), lay((None, 3, 2 * dff), (l, 0, 0)),
        lay((None, 1, 2 * dff), (l, 0, 0)), lay((None, dff, d), (l, 0, 0)), full((1, d)),
        lay((None, 2, n, 2 * dff), (l, 0, 0, 0)),
    ]
    out_shape = [jax.ShapeDtypeStruct((n, d), F32), jax.ShapeDtypeStruct((2, n, 2 * dff), F32)]
    return pl.pallas_call(
        functools.partial(_spost_kernel, dff=dff, final=final),
        grid=(1,), in_specs=in_specs, out_specs=[full(s.shape) for s in out_shape], out_shape=out_shape,
        name=f"sample_post_{l}",
        compiler_params=pltpu.CompilerParams(
            dimension_semantics=("arbitrary",), vmem_limit_bytes=VMEM_LIMIT),
    )(x, att, g0, mp, w['w_branch'], w['w_o'], w['norm2'], w['w_up'], w['fconv_w'], w['fconv_b'],
      w['w_down'], w['norm_f'], fst)


CHOOSE_SEQS = 8


def _choose_kernel(q_ref, km_ref, o_ref, *, nb):
    lane = lax.broadcasted_iota(jnp.int32, (nb, HEAD_DIM), 1)
    blk = lax.broadcasted_iota(jnp.int32, (nb, HEAD_DIM), 0)
    row = lax.broadcasted_iota(jnp.int32, (SUBLANES, HEAD_DIM), 0)
    for s in range(q_ref.shape[0]):
        gate = jnp.zeros((nb, HEAD_DIM), F32)
        for h in range(N_HEADS):
            gh = jnp.concatenate(
                [jnp.sum(km_ref[s, part, h // GROUP] * q_ref[s, h:h + 1, :], axis=1, keepdims=True)
                 for part in range(km_ref.shape[1])], axis=0)
            gate = jnp.where(lane == h, gh, gate)
        out = jnp.zeros((SUBLANES, HEAD_DIM), jnp.int32)
        for r in range(MOBA_TOPK):
            best = jnp.max(gate, axis=0, keepdims=True)
            idx = jnp.min(jnp.where(gate == best, blk, nb), axis=0, keepdims=True)
            out = jnp.where(row == r, idx, out)
            gate = jnp.where(blk == idx, -jnp.inf, gate)
        o_ref[s] = out


def _sample_choose(q, kmean_l):
    n = q.shape[0]
    parts, bps = kmean_l.shape[1], kmean_l.shape[3]
    ns = math.gcd(n, CHOOSE_SEQS)
    return pl.pallas_call(
        functools.partial(_choose_kernel, nb=parts * bps),
        grid=(n // ns,),
        in_specs=[pl.BlockSpec((ns, N_HEADS, HEAD_DIM), lambda b: (b, 0, 0)),
                  pl.BlockSpec((ns, parts, N_KV_HEADS, bps, HEAD_DIM), lambda b: (b, 0, 0, 0, 0))],
        out_specs=pl.BlockSpec((ns, SUBLANES, HEAD_DIM), lambda b: (b, 0, 0)),
        out_shape=jax.ShapeDtypeStruct((n, SUBLANES, HEAD_DIM), jnp.int32),
        name="sample_choose",
        compiler_params=pltpu.CompilerParams(dimension_semantics=("arbitrary",)),
    )(q.reshape(n, N_HEADS, HEAD_DIM), kmean_l)


def _sattn_kernel(sel_ref, pt_ref, q_ref, kn_ref, vn_ref, ck_ref, cv_ref, o_ref, kbuf, vbuf, sem,
                  *, l, n_pages):
    b = pl.program_id(0)
    n = pl.num_programs(0)
    slot = b % 2

    def copies(seq, sl):
        out = []
        for h in range(N_HEADS):
            kv = h // GROUP
            for r in range(MOBA_TOPK):
                blk = sel_ref[(seq * MOBA_TOPK + r) * N_HEADS + h]
                for p in range(PAGES_PER_BLOCK):
                    page = pt_ref[seq * n_pages + blk * PAGES_PER_BLOCK + p]
                    part = r * PAGES_PER_BLOCK + p
                    for c, (src, dst) in enumerate(((ck_ref, kbuf), (cv_ref, vbuf))):
                        out.append(pltpu.make_async_copy(
                            src.at[l, page, :, kv, :],
                            dst.at[sl, h, pl.ds(part * PAGE_SIZE, PAGE_SIZE), :],
                            sem.at[sl, c, h, part]))
        return out

    @pl.when(b == 0)
    def _():
        for cp in copies(b, 0):
            cp.start()

    for cp in copies(b, slot):
        cp.wait()
    for cp in copies(jnp.minimum(b + 1, n - 1), 1 - slot):
        cp.start()

    for h in range(N_HEADS):
        kv = h // GROUP
        q = q_ref[h:h + 1, :]
        s = jnp.sum(kbuf[slot, h] * q, axis=1, keepdims=True) * ATT_SCALE
        s_new = jnp.sum(kn_ref[kv:kv + 1, :] * q, axis=1, keepdims=True) * ATT_SCALE
        m = jnp.maximum(jnp.max(s, axis=0, keepdims=True), s_new)
        p = jnp.exp(s - m)
        p_new = jnp.exp(s_new - m)
        den = jnp.sum(p, axis=0, keepdims=True) + p_new
        num = jnp.sum(p * vbuf[slot, h], axis=0, keepdims=True) + p_new * vn_ref[kv:kv + 1, :]
        o_ref[h:h + 1, :] = num / den

    @pl.when(b == n - 1)
    def _():
        for cp in copies(b, 1 - slot):
            cp.wait()


def _sample_attn(l, sel, page_table, q, k_new, v_new, cache_k, cache_v):
    n = q.shape[0]
    n_pages = page_table.shape[1]
    rows = MOBA_TOPK * MOBA_BLOCK
    grid_spec = pltpu.PrefetchScalarGridSpec(
        num_scalar_prefetch=2,
        grid=(n,),
        in_specs=[pl.BlockSpec((None, N_HEADS, HEAD_DIM), lambda b, *_: (b, 0, 0)),
                  pl.BlockSpec((None, N_KV_HEADS, HEAD_DIM), lambda b, *_: (b, 0, 0)),
                  pl.BlockSpec((None, N_KV_HEADS, HEAD_DIM), lambda b, *_: (b, 0, 0)),
                  pl.BlockSpec(memory_space=pl.ANY),
                  pl.BlockSpec(memory_space=pl.ANY)],
        out_specs=pl.BlockSpec((None, N_HEADS, HEAD_DIM), lambda b, *_: (b, 0, 0)),
        scratch_shapes=[pltpu.VMEM((2, N_HEADS, rows, HEAD_DIM), F32),
                        pltpu.VMEM((2, N_HEADS, rows, HEAD_DIM), F32),
                        pltpu.SemaphoreType.DMA((2, 2, N_HEADS, MOBA_TOPK * PAGES_PER_BLOCK))],
    )
    sel_flat = sel[:, :MOBA_TOPK, :N_HEADS].reshape(-1)
    out = pl.pallas_call(
        functools.partial(_sattn_kernel, l=l, n_pages=n_pages),
        grid_spec=grid_spec,
        out_shape=jax.ShapeDtypeStruct((n, N_HEADS, HEAD_DIM), F32),
        name=f"sample_attn_{l}",
        compiler_params=pltpu.CompilerParams(dimension_semantics=("arbitrary",)),
    )(sel_flat, page_table.reshape(-1), q.reshape(n, N_HEADS, HEAD_DIM),
      k_new.reshape(n, N_KV_HEADS, HEAD_DIM), v_new.reshape(n, N_KV_HEADS, HEAD_DIM), cache_k, cache_v)
    return out.reshape(n, N_HEADS * HEAD_DIM)


def _trunk(x_prompt, x_sample, cache_k, cache_v, state_sconv, state_rconv, state_lru, state_fconv,
           page_table, w, tm):
    depth = w['w_in'].shape[0]
    bsz, t, d = x_prompt.shape
    n = x_sample.shape[0]
    xp = x_prompt
    xs = x_sample.reshape(n, d)
    sst = jnp.swapaxes(state_sconv, 1, 2)
    rst = jnp.swapaxes(state_rconv, 1, 2)
    fst = jnp.swapaxes(state_fconv, 1, 2)

    k_all = jnp.zeros((depth, bsz, t * N_KV_HEADS, HEAD_DIM), F32)
    v_all = jnp.zeros((depth, bsz, t * N_KV_HEADS, HEAD_DIM), F32)
    st_p, st_s = [], []
    for l in range(depth):
        final = l == depth - 1
        q, kb, vb, k_all, v_all, km, g0, mp, scn, rcn, hl = _prompt_mix(l, xp, w, k_all, v_all, tm)
        att = _prompt_attn(q, kb, vb, km.reshape(bsz, t // MOBA_BLOCK, KV_W))
        xp, fcn, kmean_s = _prompt_post(l, xp, att, g0, mp, w, cache_k, page_table, tm, final)
        st_p.append((scn, rcn, hl.reshape(bsz, d), fcn))

        qs, ks, vs, g0s, mps, scns, rcns, hls = _sample_mix(l, xs, sst, rst, state_lru, w)
        sel = _sample_choose(qs, kmean_s)
        atts = _sample_attn(l, sel, page_table, qs, ks, vs, cache_k, cache_v)
        xs, fcns = _sample_post(l, xs, atts, g0s, mps, fst, w, final)
        st_s.append((ks.reshape(n, 1, N_KV_HEADS, HEAD_DIM), vs.reshape(n, 1, N_KV_HEADS, HEAD_DIM),
                     jnp.swapaxes(scns, 0, 1), jnp.swapaxes(rcns, 0, 1), hls, jnp.swapaxes(fcns, 0, 1)))

    outs_p = [k_all.reshape(depth, bsz, t, N_KV_HEADS, HEAD_DIM), v_all.reshape(depth, bsz, t, N_KV_HEADS, HEAD_DIM)]
    outs_p += [jnp.stack([s[i] for s in st_p], axis=0) for i in range(4)]
    outs_s = [jnp.stack([s[i] for s in st_s], axis=0) for i in range(6)]
    return (xp, xs.reshape(n, 1, d), *outs_p, *outs_s)


def kernel(x_prompt, x_sample, cache_k, cache_v, state_sconv, state_rconv, state_lru, state_fconv, page_table, norm1, w_in, sconv_w, rconv_w, rconv_b, lru_wa, lru_ba, lru_wx, lru_bx, lru_lambda, w_branch, w_merge, b_merge, w_o, norm2, w_up, fconv_w, fconv_b, w_down, norm_f):
    row = lambda a: a.reshape(a.shape[0], 1, a.shape[1])
    w = dict(
        norm1=row(norm1), w_in=w_in.astype(BF16), sconv_w=sconv_w, rconv_w=rconv_w, rconv_b=row(rconv_b),
        lru_wa=lru_wa.astype(BF16), lru_ba=row(lru_ba), lru_wx=lru_wx.astype(BF16), lru_bx=row(lru_bx),
        lru_wax=jnp.concatenate([lru_wa, lru_wx], axis=-1).astype(BF16),
        lru_lambda=row(lru_lambda), w_branch=w_branch.astype(BF16), w_merge=w_merge.astype(BF16),
        b_merge=row(b_merge), w_o=w_o.astype(BF16), norm2=row(norm2), w_up=w_up.astype(BF16),
        fconv_w=fconv_w, fconv_b=row(fconv_b), w_down=w_down.astype(BF16), norm_f=norm_f.reshape(1, -1),
    )
    return _trunk(x_prompt, x_sample, cache_k, cache_v, state_sconv, state_rconv, state_lru, state_fconv,
                  page_table, w, TIME_TILE)
```

```python
import functools
import math

import jax
import jax.numpy as jnp
from jax import lax
from jax.experimental import pallas as pl
from jax.experimental.pallas import tpu as pltpu

F32 = jnp.float32
BF16 = jnp.bfloat16

HEAD_DIM = 128
N_HEADS = 8
N_KV_HEADS = 4
GROUP = N_HEADS // N_KV_HEADS
KV_W = N_KV_HEADS * HEAD_DIM
MOBA_BLOCK = 256
MOBA_TOPK = 3
LRU_BLOCKS = 8
LRU_C = 8.0
RMS_EPS = 1e-6
PAGE_SIZE = 128
PAGES_PER_BLOCK = MOBA_BLOCK // PAGE_SIZE
ATT_SCALE = HEAD_DIM ** -0.5
LOG2_E = math.log2(math.e)

SUBLANES = 8
HALO = SUBLANES
VMEM_LIMIT = 56 * 1024 * 1024
POST_VMEM_LIMIT = 60 * 1024 * 1024
TIME_TILE = 256


def _dot(a, b):
    return jnp.dot(a, b, preferred_element_type=F32)


def _dot_nt(a, b, precision=None):
    return lax.dot_general(a, b, (((1,), (1,)), ((), ())), preferred_element_type=F32,
                           precision=precision)


def _dot_tn(a, b):
    return lax.dot_general(a, b, (((0,), (0,)), ((), ())), preferred_element_type=F32)


def _rms(x, g):
    return x * lax.rsqrt(jnp.mean(x * x, axis=-1, keepdims=True) + RMS_EPS) * g


def _gelu_tanh(x):
    c = math.sqrt(2.0 / math.pi)
    return x * (0.5 * (1.0 + jnp.tanh(c * (x + 0.044715 * (x * x * x)))))


def _softplus(x):
    return jnp.maximum(x, 0.0) + jnp.log1p(jnp.exp(-jnp.abs(x)))


def _lru_coeffs(xc, r_lin, i_lin, lam):
    r = jax.nn.sigmoid(r_lin)
    i = jax.nn.sigmoid(i_lin)
    log_a = -LRU_C * r * _softplus(-lam)
    a = jnp.exp(log_a)
    th = jnp.tanh(log_a)
    b = jnp.sqrt(-2.0 * th / (1.0 - th)) * (i * xc)
    return a, b


def _const_spec(shape, index):
    return pl.BlockSpec(shape, lambda *_: index, pipeline_mode=pl.Buffered(1))


def _mix_kernel(x_ref, g1_ref, win_ref, scw_ref, rcw_ref, rcb_ref, wax_ref, ba_ref, bx_ref,
                lam_ref, wb1_ref, wb2_ref, wm_ref, bm_ref, kall_ref, vall_ref,
                q_ref, kb_ref, vb_ref, kf_ref, vf_ref, km_ref, g0_ref, mp_ref, scn_ref, rcn_ref, hl_ref,
                ext_sc, ext_rc, lin_a, lin_i, h_carry, *, tm, d):
    del kall_ref, vall_ref
    t = pl.program_id(1)
    last = pl.num_programs(1) - 1

    @pl.when(t == 0)
    def _():
        ext_sc[0:HALO, :] = jnp.zeros((HALO, d), F32)
        ext_rc[0:HALO, :] = jnp.zeros((HALO, d), F32)
        h_carry[...] = jnp.zeros((1, d), F32)

    xn = _rms(x_ref[...], g1_ref[...]).astype(BF16)
    c0 = d + 2 * KV_W
    bw = d // LRU_BLOCKS

    ext_rc[HALO:HALO + tm, :] = _dot(xn, win_ref[:, c0 + 3 * d:c0 + 4 * d])
    u = _dot(xn, win_ref[:, c0 + d:c0 + 2 * d]) * _dot(xn, win_ref[:, c0 + 2 * d:c0 + 3 * d])
    xc = (rcw_ref[0:1, :] * ext_rc[HALO - 3:HALO - 3 + tm, :]
          + rcw_ref[1:2, :] * ext_rc[HALO - 2:HALO - 2 + tm, :]
          + rcw_ref[2:3, :] * ext_rc[HALO - 1:HALO - 1 + tm, :]
          + rcw_ref[3:4, :] * ext_rc[HALO:HALO + tm, :]) + rcb_ref[...]
    xcb = xc.astype(BF16)
    for n in range(LRU_BLOCKS):
        both = _dot(xcb[:, n * bw:(n + 1) * bw], wax_ref[n])
        lin_a[:, n * bw:(n + 1) * bw] = both[:, 0:bw]
        lin_i[:, n * bw:(n + 1) * bw] = both[:, bw:2 * bw]

    b_gate = _dot(xn, win_ref[:, c0:c0 + d])
    qg = GROUP * HEAD_DIM
    for h in range(N_KV_HEADS):
        q_ref[h] = _dot(xn, win_ref[:, h * qg:(h + 1) * qg]).astype(BF16)
    ext_sc[HALO:HALO + tm, :] = u
    sc = (scw_ref[0:1, :] * ext_sc[HALO - 2:HALO - 2 + tm, :]
          + scw_ref[1:2, :] * ext_sc[HALO - 1:HALO - 1 + tm, :]
          + scw_ref[2:3, :] * u)
    o_sc = (b_gate * sc).astype(BF16)

    a, b = _lru_coeffs(xc, lin_a[...] + ba_ref[...], lin_i[...] + bx_ref[...], lam_ref[...])
    lin_a[...] = a
    lin_i[...] = b

    k = _dot(xn, win_ref[:, d:d + KV_W])
    v = _dot(xn, win_ref[:, d + KV_W:d + 2 * KV_W])
    for h in range(N_KV_HEADS):
        kb_ref[h] = k[:, h * HEAD_DIM:(h + 1) * HEAD_DIM].astype(BF16)
        vb_ref[h] = v[:, h * HEAD_DIM:(h + 1) * HEAD_DIM].astype(BF16)
        kf_ref[pl.ds(h, tm, stride=N_KV_HEADS), :] = k[:, h * HEAD_DIM:(h + 1) * HEAD_DIM]
        vf_ref[pl.ds(h, tm, stride=N_KV_HEADS), :] = v[:, h * HEAD_DIM:(h + 1) * HEAD_DIM]
    for j in range(tm // MOBA_BLOCK):
        km_ref[j:j + 1, :] = jnp.mean(k[j * MOBA_BLOCK:(j + 1) * MOBA_BLOCK], axis=0, keepdims=True)
    g0_ref[...] = jax.nn.sigmoid(_dot(xn, wm_ref[:, 0:d]) + bm_ref[:, 0:d])

    row = lax.broadcasted_iota(jnp.int32, (SUBLANES, d), 0)
    h_prev = jnp.broadcast_to(h_carry[...], (SUBLANES, d))
    for g in range(tm // SUBLANES):
        rows = slice(g * SUBLANES, (g + 1) * SUBLANES)
        ag = lin_a[rows, :]
        bg = lin_i[rows, :]
        s = 1
        while s < SUBLANES:
            keep = row >= s
            bg = bg + ag * jnp.where(keep, pltpu.roll(bg, s, 0), 0.0)
            ag = ag * jnp.where(keep, pltpu.roll(ag, s, 0), 1.0)
            s *= 2
        hg = ag * h_prev + bg
        lin_i[rows, :] = hg
        h_prev = jnp.broadcast_to(hg[SUBLANES - 1:SUBLANES, :], (SUBLANES, d))
    h_carry[...] = h_prev[0:1, :]

    pb1 = _dot(o_sc, wb1_ref[...])
    g_lru = _dot(xn, win_ref[:, c0 + 4 * d:c0 + 5 * d])
    mp = jax.nn.sigmoid(_dot(xn, wm_ref[:, d:2 * d]) + bm_ref[:, d:2 * d]) * pb1
    gate2 = jax.nn.sigmoid(_dot(xn, wm_ref[:, 2 * d:3 * d]) + bm_ref[:, 2 * d:3 * d])
    o_lru = (lin_i[...] * _gelu_tanh(g_lru)).astype(BF16)
    mp_ref[...] = mp + gate2 * _dot(o_lru, wb2_ref[...])

    @pl.when(t == last)
    def _():
        scn_ref[...] = ext_sc[HALO + tm - 2:HALO + tm, :]
        rcn_ref[...] = ext_rc[HALO + tm - 3:HALO + tm, :]
        hl_ref[...] = h_carry[...]

    ext_sc[HALO - 2:HALO, :] = ext_sc[HALO + tm - 2:HALO + tm, :]
    ext_rc[HALO - 3:HALO, :] = ext_rc[HALO + tm - 3:HALO + tm, :]


def _prompt_mix(l, x, w, k_all, v_all, tm):
    bsz, t, d = x.shape
    nt = t // tm
    nmb = tm // MOBA_BLOCK
    n_in = w['w_in'].shape[-1]
    row = lambda b, i: (b, i, 0)
    per_b = lambda b, i: (b, 0, 0)
    vec = lambda n: _const_spec((None, 1, n), (l, 0, 0))
    in_specs = [
        pl.BlockSpec((None, tm, d), row),
        vec(d),
        _const_spec((None, d, n_in), (l, 0, 0)),
        _const_spec((None, 3, d), (l, 0, 0)),
        _const_spec((None, 4, d), (l, 0, 0)),
        vec(d),
        _const_spec((None, LRU_BLOCKS, d // LRU_BLOCKS, 2 * d // LRU_BLOCKS), (l, 0, 0, 0)),
        vec(d),
        vec(d),
        vec(d),
        _const_spec((None, None, d, d), (l, 1, 0, 0)),
        _const_spec((None, None, d, d), (l, 2, 0, 0)),
        _const_spec((None, d, 3 * d), (l, 0, 0)),
        vec(3 * d),
        pl.BlockSpec(memory_space=pl.ANY),
        pl.BlockSpec(memory_space=pl.ANY),
    ]
    out_shape = [
        jax.ShapeDtypeStruct((bsz, N_KV_HEADS, t, d // N_KV_HEADS), BF16),
        jax.ShapeDtypeStruct((bsz, N_KV_HEADS, t, HEAD_DIM), BF16),
        jax.ShapeDtypeStruct((bsz, N_KV_HEADS, t, HEAD_DIM), BF16),
        jax.ShapeDtypeStruct(k_all.shape, F32),
        jax.ShapeDtypeStruct(v_all.shape, F32),
        jax.ShapeDtypeStruct((bsz, nt, nmb, KV_W), F32),
        jax.ShapeDtypeStruct((bsz, t, d), F32),
        jax.ShapeDtypeStruct((bsz, t, d), F32),
        jax.ShapeDtypeStruct((bsz, 2, d), F32),
        jax.ShapeDtypeStruct((bsz, 3, d), F32),
        jax.ShapeDtypeStruct((bsz, 1, d), F32),
    ]
    head_major = lambda b, i: (b, 0, i, 0)
    out_specs = [
        pl.BlockSpec((None, N_KV_HEADS, tm, d // N_KV_HEADS), head_major),
        pl.BlockSpec((None, N_KV_HEADS, tm, HEAD_DIM), head_major),
        pl.BlockSpec((None, N_KV_HEADS, tm, HEAD_DIM), head_major),
        pl.BlockSpec((None, None, tm * N_KV_HEADS, HEAD_DIM), lambda b, i: (l, b, i, 0)),
        pl.BlockSpec((None, None, tm * N_KV_HEADS, HEAD_DIM), lambda b, i: (l, b, i, 0)),
        pl.BlockSpec((None, None, nmb, KV_W), lambda b, i: (b, i, 0, 0)),
        pl.BlockSpec((None, tm, d), row),
        pl.BlockSpec((None, tm, d), row),
        pl.BlockSpec((None, 2, d), per_b),
        pl.BlockSpec((None, 3, d), per_b),
        pl.BlockSpec((None, 1, d), per_b),
    ]
    scratch = [
        pltpu.VMEM((HALO + tm, d), F32),
        pltpu.VMEM((HALO + tm, d), F32),
        pltpu.VMEM((tm, d), F32),
        pltpu.VMEM((tm, d), F32),
        pltpu.VMEM((1, d), F32),
    ]
    return pl.pallas_call(
        functools.partial(_mix_kernel, tm=tm, d=d),
        grid=(bsz, nt), in_specs=in_specs, out_specs=out_specs, out_shape=out_shape,
        scratch_shapes=scratch, name=f"prompt_mix_{l}",
        input_output_aliases={len(in_specs) - 2: 3, len(in_specs) - 1: 4},
        compiler_params=pltpu.CompilerParams(
            dimension_semantics=("arbitrary", "arbitrary"), vmem_limit_bytes=VMEM_LIMIT),
    )(x, w['norm1'], w['w_in'], w['sconv_w'], w['rconv_w'], w['rconv_b'], w['lru_wax'], w['lru_ba'],
      w['lru_bx'], w['lru_lambda'], w['w_branch'], w['w_branch'], w['w_merge'], w['b_merge'], k_all, v_all)


def _attn_kernel(q_ref, k_ref, v_ref, km_ref, o_ref, *, nb):
    blk = MOBA_BLOCK
    nq = GROUP * blk

    parts = []
    rest = km_ref[...]
    for _ in range(3):
        part = rest.astype(BF16).astype(F32)
        parts += [part, jnp.zeros_like(part)]
        rest = rest - part
    km_terms = jnp.concatenate(parts, axis=0).astype(BF16)
    prow = 2 * nb

    block_id = lax.broadcasted_iota(jnp.int32, (nb, nq), 0)
    key = lax.broadcasted_iota(jnp.int32, (blk, nq), 0)
    qpos = lax.broadcasted_iota(jnp.int32, (blk, nq), 1) & (blk - 1)
    causal = key <= qpos

    def scores(i):
        q2 = jnp.concatenate([q_ref[i * blk:(i + 1) * blk, g * HEAD_DIM:(g + 1) * HEAD_DIM]
                              for g in range(GROUP)], axis=0)
        keys = k_ref[0:(i + 1) * blk, :]
        return _dot_nt(keys if i == 0 else jnp.concatenate([keys, km_terms], axis=0), q2)

    s_next = scores(0)
    for i in range(nb):
        q_rows = slice(i * blk, (i + 1) * blk)
        n_keys = (i + 1) * blk
        s_all = s_next
        if i + 1 < nb:
            s_next = scores(i + 1)
        if i > 0:
            gate = (s_all[n_keys + 2 * prow:n_keys + 2 * prow + nb]
                    + s_all[n_keys + prow:n_keys + prow + nb] + s_all[n_keys:n_keys + nb])
            past = block_id < i
            gate = jnp.where(past, gate, -jnp.inf)
            rank = jnp.zeros((nb, nq), jnp.int32)
            for j in range(i):
                gj = gate[j:j + 1, :]
                rank = rank + jnp.where((gj > gate) | ((gj == gate) & (j < block_id)), 1, 0)
            sel = jnp.where(past & (rank < MOBA_TOPK), 1.0, 0.0)

        pieces = [jnp.where(sel[j:j + 1, :] > 0.5, s_all[j * blk:(j + 1) * blk], -jnp.inf) for j in range(i)]
        pieces.append(jnp.where(causal, s_all[i * blk:n_keys], -jnp.inf))
        m = jnp.max(pieces[-1], axis=0, keepdims=True)
        for piece in pieces[:-1]:
            m = jnp.maximum(m, jnp.max(piece, axis=0, keepdims=True))
        probs = [jnp.exp2((piece - m) * (ATT_SCALE * LOG2_E)) for piece in pieces]
        den = jnp.sum(probs[0], axis=0, keepdims=True)
        for pr in probs[1:]:
            den = den + jnp.sum(pr, axis=0, keepdims=True)
        p_all = jnp.concatenate([pr.astype(BF16) for pr in probs], axis=0)
        out = (_dot_tn(v_ref[0:n_keys, :], p_all) / den).T
        for g in range(GROUP):
            o_ref[q_rows, g * HEAD_DIM:(g + 1) * HEAD_DIM] = out[g * blk:(g + 1) * blk].astype(o_ref.dtype)


def _prompt_attn(q, k, v, kmean):
    bsz, _, t, gw = q.shape
    nb = t // MOBA_BLOCK
    return pl.pallas_call(
        functools.partial(_attn_kernel, nb=nb),
        grid=(bsz, N_KV_HEADS),
        in_specs=[
            pl.BlockSpec((None, None, t, gw), lambda b, h: (b, h, 0, 0)),
            pl.BlockSpec((None, None, t, HEAD_DIM), lambda b, h: (b, h, 0, 0)),
            pl.BlockSpec((None, None, t, HEAD_DIM), lambda b, h: (b, h, 0, 0)),
            pl.BlockSpec((None, nb, HEAD_DIM), lambda b, h: (b, 0, h)),
        ],
        out_specs=pl.BlockSpec((None, None, t, gw), lambda b, h: (b, h, 0, 0)),
        out_shape=jax.ShapeDtypeStruct((bsz, N_KV_HEADS, t, gw), BF16),
        name="prompt_attn",
        compiler_params=pltpu.CompilerParams(
            dimension_semantics=("arbitrary", "arbitrary"), vmem_limit_bytes=VMEM_LIMIT),
    )(q, k, v, kmean)


def _post_kernel(pt_ref, x_ref, att_ref, g0_ref, mp_ref, wb0_ref, wo_ref, g2_ref, wup_ref, fcw_ref, fcb_ref,
                 wdn_ref, gf_ref, cache_ref, y_ref, fcn_ref, km_ref, ext_fc, cbuf, csem,
                 *, layer, tm, d, dff, final, pps):
    t = pl.program_id(1)
    last = pl.num_programs(1) - 1
    step = pl.program_id(0) * pl.num_programs(1) + t
    n_steps = pl.num_programs(0) * pl.num_programs(1)
    slot = step % 2

    def page_copies(s, sl):
        return [pltpu.make_async_copy(cache_ref.at[layer, pt_ref[s * pps + p]], cbuf.at[sl, p], csem.at[sl, p])
                for p in range(pps)]

    @pl.when(step == 0)
    def _():
        for cp in page_copies(step, 0):
            cp.start()

    for cp in page_copies(step, slot):
        cp.wait()

    att = jnp.concatenate([att_ref[h] for h in range(N_KV_HEADS)], axis=1)
    merged = g0_ref[...] * _dot(att, wb0_ref[...]) + mp_ref[...]
    x1 = x_ref[...] + _dot(merged.astype(BF16), wo_ref[...])

    for cp in page_copies(jnp.minimum(step + 1, n_steps - 1), 1 - slot):
        cp.start()
    groups = PAGE_SIZE * N_KV_HEADS // SUBLANES
    for j in range(pps // PAGES_PER_BLOCK):
        tot = jnp.zeros((SUBLANES, HEAD_DIM), F32)
        for p in range(PAGES_PER_BLOCK):
            page = cbuf.at[slot, PAGES_PER_BLOCK * j + p]
            for g in range(groups):
                tot = tot + page[g * SUBLANES:(g + 1) * SUBLANES, :]
        tot = (tot[0:N_KV_HEADS] + tot[N_KV_HEADS:2 * N_KV_HEADS]) * (1.0 / MOBA_BLOCK)
        for h in range(N_KV_HEADS):
            km_ref[h, j:j + 1, :] = tot[h:h + 1, :]

    @pl.when(t == 0)
    def _():
        ext_fc[0:HALO, :] = jnp.zeros((HALO, 2 * dff), F32)

    xn2 = _rms(x1, g2_ref[...]).astype(BF16)
    ext_fc[HALO:HALO + tm, :] = _dot(xn2, wup_ref[...])

    def conv(c0, c1):
        return (fcw_ref[0:1, c0:c1] * ext_fc[HALO - 2:HALO - 2 + tm, c0:c1]
                + fcw_ref[1:2, c0:c1] * ext_fc[HALO - 1:HALO - 1 + tm, c0:c1]
                + fcw_ref[2:3, c0:c1] * ext_fc[HALO:HALO + tm, c0:c1]) + fcb_ref[:, c0:c1]

    hh = (jax.nn.silu(conv(0, dff)) * conv(dff, 2 * dff)).astype(BF16)
    x2 = x1 + _dot(hh, wdn_ref[...])
    y_ref[...] = _rms(x2, gf_ref[...]) if final else x2

    @pl.when(t == last)
    def _():
        fcn_ref[...] = ext_fc[HALO + tm - 2:HALO + tm, :]

    ext_fc[HALO - 2:HALO, :] = ext_fc[HALO + tm - 2:HALO + tm, :]

    @pl.when(step == n_steps - 1)
    def _():
        for cp in page_copies(step, 1 - slot):
            cp.wait()


def _prompt_post(l, x, att, g0, mp, w, cache_k, page_table, tm, final):
    bsz, t, d = x.shape
    dff = w['w_down'].shape[1]
    depth, n_pool, page, hkv, hd = cache_k.shape
    n_b, n_pages = page_table.shape
    n_steps = bsz * (t // tm)
    pps = n_b * n_pages // n_steps
    assert pps * n_steps == n_b * n_pages and n_pages % pps == 0
    assert pps % (PAGES_PER_BLOCK * SUBLANES) == 0
    bps = pps // PAGES_PER_BLOCK
    nt = t // tm
    row = lambda b, i, pt: (b, i, 0)
    tile = pl.BlockSpec((None, tm, d), row)
    in_specs = [
        tile, pl.BlockSpec((None, N_KV_HEADS, tm, d // N_KV_HEADS), lambda b, i, pt: (b, 0, i, 0)), tile, tile,
        _const_spec((None, None, d, d), (l, 0, 0, 0)),
        _const_spec((None, d, d), (l, 0, 0)),
        _const_spec((None, 1, d), (l, 0, 0)),
        _const_spec((None, d, 2 * dff), (l, 0, 0)),
        _const_spec((None, 3, 2 * dff), (l, 0, 0)),
        _const_spec((None, 1, 2 * dff), (l, 0, 0)),
        _const_spec((None, dff, d), (l, 0, 0)),
        _const_spec((1, d), (0, 0)),
        pl.BlockSpec(memory_space=pl.ANY),
    ]
    grid_spec = pltpu.PrefetchScalarGridSpec(
        num_scalar_prefetch=1,
        grid=(bsz, nt),
        in_specs=in_specs,
        out_specs=[tile,
                   pl.BlockSpec((None, 2, 2 * dff), lambda b, i, pt: (b, 0, 0)),
                   pl.BlockSpec((None, hkv, bps, hd), lambda b, i, pt: (b * nt + i, 0, 0, 0))],
        scratch_shapes=[pltpu.VMEM((HALO + tm, 2 * dff), F32),
                        pltpu.VMEM((2, pps, page * hkv, hd), F32),
                        pltpu.SemaphoreType.DMA((2, pps))],
    )
    y, fcn, km = pl.pallas_call(
        functools.partial(_post_kernel, layer=l, tm=tm, d=d, dff=dff, final=final, pps=pps),
        grid_spec=grid_spec,
        out_shape=[jax.ShapeDtypeStruct((bsz, t, d), F32),
                   jax.ShapeDtypeStruct((bsz, 2, 2 * dff), F32),
                   jax.ShapeDtypeStruct((n_steps, hkv, bps, hd), F32)],
        name=f"prompt_post_{l}",
        compiler_params=pltpu.CompilerParams(
            dimension_semantics=("arbitrary", "arbitrary"), vmem_limit_bytes=POST_VMEM_LIMIT),
    )(page_table.reshape(-1), x, att, g0, mp, w['w_branch'], w['w_o'], w['norm2'], w['w_up'], w['fconv_w'],
      w['fconv_b'], w['w_down'], w['norm_f'], cache_k.reshape(depth, n_pool, page * hkv, hd))
    return y, fcn, km.reshape(n_b, n_pages // pps, hkv, bps, hd)


def _smix_kernel(x_ref, g1_ref, win_ref, scw_ref, rcw_ref, rcb_ref, wa_ref, ba_ref, wx_ref, bx_ref,
                 lam_ref, wb1_ref, wb2_ref, wm_ref, bm_ref, sst_ref, rst_ref, h0_ref,
                 q_ref, k_ref, v_ref, g0_ref, mp_ref, scn_ref, rcn_ref, hl_ref, *, d):
    xn = _rms(x_ref[...], g1_ref[...]).astype(BF16)
    q_ref[...] = _dot(xn, win_ref[:, 0:d])
    k_ref[...] = _dot(xn, win_ref[:, d:d + KV_W])
    v_ref[...] = _dot(xn, win_ref[:, d + KV_W:d + 2 * KV_W])
    c0 = d + 2 * KV_W

    u = _dot(xn, win_ref[:, c0 + d:c0 + 2 * d]) * _dot(xn, win_ref[:, c0 + 2 * d:c0 + 3 * d])
    sc = scw_ref[0:1, :] * sst_ref[0] + scw_ref[1:2, :] * sst_ref[1] + scw_ref[2:3, :] * u
    scn_ref[0] = sst_ref[1]
    scn_ref[1] = u
    o_sc = (_dot(xn, win_ref[:, c0:c0 + d]) * sc).astype(BF16)
    pb1 = _dot(o_sc, wb1_ref[...])

    x_lru = _dot(xn, win_ref[:, c0 + 3 * d:c0 + 4 * d])
    xc = (rcw_ref[0:1, :] * rst_ref[0] + rcw_ref[1:2, :] * rst_ref[1] + rcw_ref[2:3, :] * rst_ref[2]
          + rcw_ref[3:4, :] * x_lru) + rcb_ref[...]
    rcn_ref[0] = rst_ref[1]
    rcn_ref[1] = rst_ref[2]
    rcn_ref[2] = x_lru
    bw = d // LRU_BLOCKS
    xcb = xc.astype(BF16)
    r_lin = jnp.concatenate(
        [_dot(xcb[:, n * bw:(n + 1) * bw], wa_ref[n]) for n in range(LRU_BLOCKS)], axis=1)
    i_lin = jnp.concatenate(
        [_dot(xcb[:, n * bw:(n + 1) * bw], wx_ref[n]) for n in range(LRU_BLOCKS)], axis=1)
    a, b = _lru_coeffs(xc, r_lin + ba_ref[...], i_lin + bx_ref[...], lam_ref[...])
    h = a * h0_ref[...] + b
    hl_ref[...] = h
    o_lru = (h * _gelu_tanh(_dot(xn, win_ref[:, c0 + 4 * d:c0 + 5 * d]))).astype(BF16)
    pb2 = _dot(o_lru, wb2_ref[...])

    g0_ref[...] = jax.nn.sigmoid(_dot(xn, wm_ref[:, 0:d]) + bm_ref[:, 0:d])
    mp_ref[...] = (jax.nn.sigmoid(_dot(xn, wm_ref[:, d:2 * d]) + bm_ref[:, d:2 * d]) * pb1
                   + jax.nn.sigmoid(_dot(xn, wm_ref[:, 2 * d:3 * d]) + bm_ref[:, 2 * d:3 * d]) * pb2)


def _sample_mix(l, x, sst, rst, h0, w):
    n, d = x.shape
    n_in = w['w_in'].shape[-1]
    full = lambda shape: pl.BlockSpec(shape, lambda i: (0,) * len(shape))
    lay = lambda shape, idx: pl.BlockSpec(shape, lambda i: idx)
    vec = lambda m: lay((None, 1, m), (l, 0, 0))
    lw = (None, LRU_BLOCKS, d // LRU_BLOCKS, d // LRU_BLOCKS)
    in_specs = [
        full((n, d)), vec(d), lay((None, d, n_in), (l, 0, 0)), lay((None, 3, d), (l, 0, 0)),
        lay((None, 4, d), (l, 0, 0)), vec(d), lay(lw, (l, 0, 0, 0)), vec(d), lay(lw, (l, 0, 0, 0)),
        vec(d), vec(d), lay((None, None, d, d), (l, 1, 0, 0)), lay((None, None, d, d), (l, 2, 0, 0)),
        lay((None, d, 3 * d), (l, 0, 0)), vec(3 * d),
        lay((None, 2, n, d), (l, 0, 0, 0)), lay((None, 3, n, d), (l, 0, 0, 0)), lay((None, n, d), (l, 0, 0)),
    ]
    out_shape = [
        jax.ShapeDtypeStruct((n, d), F32), jax.ShapeDtypeStruct((n, KV_W), F32),
        jax.ShapeDtypeStruct((n, KV_W), F32), jax.ShapeDtypeStruct((n, d), F32),
        jax.ShapeDtypeStruct((n, d), F32), jax.ShapeDtypeStruct((2, n, d), F32),
        jax.ShapeDtypeStruct((3, n, d), F32), jax.ShapeDtypeStruct((n, d), F32),
    ]
    out_specs = [full(s.shape) for s in out_shape]
    return pl.pallas_call(
        functools.partial(_smix_kernel, d=d),
        grid=(1,), in_specs=in_specs, out_specs=out_specs, out_shape=out_shape,
        name=f"sample_mix_{l}",
        compiler_params=pltpu.CompilerParams(
            dimension_semantics=("arbitrary",), vmem_limit_bytes=VMEM_LIMIT),
    )(x, w['norm1'], w['w_in'], w['sconv_w'], w['rconv_w'], w['rconv_b'], w['lru_wa'], w['lru_ba'],
      w['lru_wx'], w['lru_bx'], w['lru_lambda'], w['w_branch'], w['w_branch'], w['w_merge'],
      w['b_merge'], sst, rst, h0)


def _spost_kernel(x_ref, att_ref, g0_ref, mp_ref, wb0_ref, wo_ref, g2_ref, wup_ref, fcw_ref, fcb_ref,
                  wdn_ref, gf_ref, fst_ref, y_ref, fcn_ref, *, dff, final):
    merged = g0_ref[...] * _dot(att_ref[...].astype(BF16), wb0_ref[...]) + mp_ref[...]
    x1 = x_ref[...] + _dot(merged.astype(BF16), wo_ref[...])
    xn2 = _rms(x1, g2_ref[...]).astype(BF16)
    up = _dot(xn2, wup_ref[...])
    c = (fcw_ref[0:1, :] * fst_ref[0] + fcw_ref[1:2, :] * fst_ref[1] + fcw_ref[2:3, :] * up) + fcb_ref[...]
    fcn_ref[0] = fst_ref[1]
    fcn_ref[1] = up
    hh = (jax.nn.silu(c[:, 0:dff]) * c[:, dff:2 * dff]).astype(BF16)
    x2 = x1 + _dot(hh, wdn_ref[...])
    y_ref[...] = _rms(x2, gf_ref[...]) if final else x2


def _sample_post(l, x, att, g0, mp, fst, w, final):
    n, d = x.shape
    dff = w['w_down'].shape[1]
    full = lambda shape: pl.BlockSpec(shape, lambda i: (0,) * len(shape))
    lay = lambda shape, idx: pl.BlockSpec(shape, lambda i: idx)
    in_specs = [
        full((n, d)), full((n, d)), full((n, d)), full((n, d)),
        lay((None, None, d, d), (l, 0, 0, 0)), lay((None, d, d), (l, 0, 0)), lay((None, 1, d), (l, 0, 0)),
        lay((None, d, 2 * dff), (l, 0, 0)), lay((None, 3, 2 * dff), (l, 0, 0)),
        lay((None, 1, 2 * dff), (l, 0, 0)), lay((None, dff, d), (l, 0, 0)), full((1, d)),
        lay((None, 2, n, 2 * dff), (l, 0, 0, 0)),
    ]
    out_shape = [jax.ShapeDtypeStruct((n, d), F32), jax.ShapeDtypeStruct((2, n, 2 * dff), F32)]
    return pl.pallas_call(
        functools.partial(_spost_kernel, dff=dff, final=final),
        grid=(1,), in_specs=in_specs, out_specs=[full(s.shape) for s in out_shape], out_shape=out_shape,
        name=f"sample_post_{l}",
        compiler_params=pltpu.CompilerParams(
            dimension_semantics=("arbitrary",), vmem_limit_bytes=VMEM_LIMIT),
    )(x, att, g0, mp, w['w_branch'], w['w_o'], w['norm2'], w['w_up'], w['fconv_w'], w['fconv_b'],
      w['w_down'], w['norm_f'], fst)


CHOOSE_SEQS = 8


def _choose_kernel(q_ref, km_ref, o_ref, *, nb):
    lane = lax.broadcasted_iota(jnp.int32, (nb, HEAD_DIM), 1)
    blk = lax.broadcasted_iota(jnp.int32, (nb, HEAD_DIM), 0)
    row = lax.broadcasted_iota(jnp.int32, (SUBLANES, HEAD_DIM), 0)
    for s in range(q_ref.shape[0]):
        gate = jnp.zeros((nb, HEAD_DIM), F32)
        for h in range(N_HEADS):
            gh = jnp.concatenate(
                [jnp.sum(km_ref[s, part, h // GROUP] * q_ref[s, h:h + 1, :], axis=1, keepdims=True)
                 for part in range(km_ref.shape[1])], axis=0)
            gate = jnp.where(lane == h, gh, gate)
        out = jnp.zeros((SUBLANES, HEAD_DIM), jnp.int32)
        for r in range(MOBA_TOPK):
            best = jnp.max(gate, axis=0, keepdims=True)
            idx = jnp.min(jnp.where(gate == best, blk, nb), axis=0, keepdims=True)
            out = jnp.where(row == r, idx, out)
            gate = jnp.where(blk == idx, -jnp.inf, gate)
        o_ref[s] = out


def _sample_choose(q, kmean_l):
    n = q.shape[0]
    parts, bps = kmean_l.shape[1], kmean_l.shape[3]
    ns = math.gcd(n, CHOOSE_SEQS)
    return pl.pallas_call(
        functools.partial(_choose_kernel, nb=parts * bps),
        grid=(n // ns,),
        in_specs=[pl.BlockSpec((ns, N_HEADS, HEAD_DIM), lambda b: (b, 0, 0)),
                  pl.BlockSpec((ns, parts, N_KV_HEADS, bps, HEAD_DIM), lambda b: (b, 0, 0, 0, 0))],
        out_specs=pl.BlockSpec((ns, SUBLANES, HEAD_DIM), lambda b: (b, 0, 0)),
        out_shape=jax.ShapeDtypeStruct((n, SUBLANES, HEAD_DIM), jnp.int32),
        name="sample_choose",
        compiler_params=pltpu.CompilerParams(dimension_semantics=("arbitrary",)),
    )(q.reshape(n, N_HEADS, HEAD_DIM), kmean_l)


def _sattn_kernel(sel_ref, pt_ref, q_ref, kn_ref, vn_ref, ck_ref, cv_ref, o_ref, kbuf, vbuf, sem,
                  *, l, n_pages):
    b = pl.program_id(0)
    n = pl.num_programs(0)
    slot = b % 2

    def copies(seq, sl):
        out = []
        for h in range(N_HEADS):
            kv = h // GROUP
            for r in range(MOBA_TOPK):
                blk = sel_ref[(seq * MOBA_TOPK + r) * N_HEADS + h]
                for p in range(PAGES_PER_BLOCK):
                    page = pt_ref[seq * n_pages + blk * PAGES_PER_BLOCK + p]
                    part = r * PAGES_PER_BLOCK + p
                    for c, (src, dst) in enumerate(((ck_ref, kbuf), (cv_ref, vbuf))):
                        out.append(pltpu.make_async_copy(
                            src.at[l, page, :, kv, :],
                            dst.at[sl, h, pl.ds(part * PAGE_SIZE, PAGE_SIZE), :],
                            sem.at[sl, c, h, part]))
        return out

    @pl.when(b == 0)
    def _():
        for cp in copies(b, 0):
            cp.start()

    for cp in copies(b, slot):
        cp.wait()
    for cp in copies(jnp.minimum(b + 1, n - 1), 1 - slot):
        cp.start()

    for h in range(N_HEADS):
        kv = h // GROUP
        q = q_ref[h:h + 1, :]
        s = jnp.sum(kbuf[slot, h] * q, axis=1, keepdims=True) * ATT_SCALE
        s_new = jnp.sum(kn_ref[kv:kv + 1, :] * q, axis=1, keepdims=True) * ATT_SCALE
        m = jnp.maximum(jnp.max(s, axis=0, keepdims=True), s_new)
        p = jnp.exp(s - m)
        p_new = jnp.exp(s_new - m)
        den = jnp.sum(p, axis=0, keepdims=True) + p_new
        num = jnp.sum(p * vbuf[slot, h], axis=0, keepdims=True) + p_new * vn_ref[kv:kv + 1, :]
        o_ref[h:h + 1, :] = num / den

    @pl.when(b == n - 1)
    def _():
        for cp in copies(b, 1 - slot):
            cp.wait()


def _sample_attn(l, sel, page_table, q, k_new, v_new, cache_k, cache_v):
    n = q.shape[0]
    n_pages = page_table.shape[1]
    rows = MOBA_TOPK * MOBA_BLOCK
    grid_spec = pltpu.PrefetchScalarGridSpec(
        num_scalar_prefetch=2,
        grid=(n,),
        in_specs=[pl.BlockSpec((None, N_HEADS, HEAD_DIM), lambda b, *_: (b, 0, 0)),
                  pl.BlockSpec((None, N_KV_HEADS, HEAD_DIM), lambda b, *_: (b, 0, 0)),
                  pl.BlockSpec((None, N_KV_HEADS, HEAD_DIM), lambda b, *_: (b, 0, 0)),
                  pl.BlockSpec(memory_space=pl.ANY),
                  pl.BlockSpec(memory_space=pl.ANY)],
        out_specs=pl.BlockSpec((None, N_HEADS, HEAD_DIM), lambda b, *_: (b, 0, 0)),
        scratch_shapes=[pltpu.VMEM((2, N_HEADS, rows, HEAD_DIM), F32),
                        pltpu.VMEM((2, N_HEADS, rows, HEAD_DIM), F32),
                        pltpu.SemaphoreType.DMA((2, 2, N_HEADS, MOBA_TOPK * PAGES_PER_BLOCK))],
    )
    sel_flat = sel[:, :MOBA_TOPK, :N_HEADS].reshape(-1)
    out = pl.pallas_call(
        functools.partial(_sattn_kernel, l=l, n_pages=n_pages),
        grid_spec=grid_spec,
        out_shape=jax.ShapeDtypeStruct((n, N_HEADS, HEAD_DIM), F32),
        name=f"sample_attn_{l}",
        compiler_params=pltpu.CompilerParams(dimension_semantics=("arbitrary",)),
    )(sel_flat, page_table.reshape(-1), q.reshape(n, N_HEADS, HEAD_DIM),
      k_new.reshape(n, N_KV_HEADS, HEAD_DIM), v_new.reshape(n, N_KV_HEADS, HEAD_DIM), cache_k, cache_v)
    return out.reshape(n, N_HEADS * HEAD_DIM)


def _trunk(x_prompt, x_sample, cache_k, cache_v, state_sconv, state_rconv, state_lru, state_fconv,
           page_table, w, tm):
    depth = w['w_in'].shape[0]
    bsz, t, d = x_prompt.shape
    n = x_sample.shape[0]
    xp = x_prompt
    xs = x_sample.reshape(n, d)
    sst = jnp.swapaxes(state_sconv, 1, 2)
    rst = jnp.swapaxes(state_rconv, 1, 2)
    fst = jnp.swapaxes(state_fconv, 1, 2)

    k_all = jnp.zeros((depth, bsz, t * N_KV_HEADS, HEAD_DIM), F32)
    v_all = jnp.zeros((depth, bsz, t * N_KV_HEADS, HEAD_DIM), F32)
    st_p, st_s = [], []
    for l in range(depth):
        final = l == depth - 1
        q, kb, vb, k_all, v_all, km, g0, mp, scn, rcn, hl = _prompt_mix(l, xp, w, k_all, v_all, tm)
        att = _prompt_attn(q, kb, vb, km.reshape(bsz, t // MOBA_BLOCK, KV_W))
        xp, fcn, kmean_s = _prompt_post(l, xp, att, g0, mp, w, cache_k, page_table, tm, final)
        st_p.append((scn, rcn, hl.reshape(bsz, d), fcn))

        qs, ks, vs, g0s, mps, scns, rcns, hls = _sample_mix(l, xs, sst, rst, state_lru, w)
        sel = _sample_choose(qs, kmean_s)
        atts = _sample_attn(l, sel, page_table, qs, ks, vs, cache_k, cache_v)
        xs, fcns = _sample_post(l, xs, atts, g0s, mps, fst, w, final)
        st_s.append((ks.reshape(n, 1, N_KV_HEADS, HEAD_DIM), vs.reshape(n, 1, N_KV_HEADS, HEAD_DIM),
                     jnp.swapaxes(scns, 0, 1), jnp.swapaxes(rcns, 0, 1), hls, jnp.swapaxes(fcns, 0, 1)))

    outs_p = [k_all.reshape(depth, bsz, t, N_KV_HEADS, HEAD_DIM), v_all.reshape(depth, bsz, t, N_KV_HEADS, HEAD_DIM)]
    outs_p += [jnp.stack([s[i] for s in st_p], axis=0) for i in range(4)]
    outs_s = [jnp.stack([s[i] for s in st_s], axis=0) for i in range(6)]
    return (xp, xs.reshape(n, 1, d), *outs_p, *outs_s)


def kernel(x_prompt, x_sample, cache_k, cache_v, state_sconv, state_rconv, state_lru, state_fconv, page_table, norm1, w_in, sconv_w, rconv_w, rconv_b, lru_wa, lru_ba, lru_wx, lru_bx, lru_lambda, w_branch, w_merge, b_merge, w_o, norm2, w_up, fconv_w, fconv_b, w_down, norm_f):
    row = lambda a: a.reshape(a.shape[0], 1, a.shape[1])
    w = dict(
        norm1=row(norm1), w_in=w_in.astype(BF16), sconv_w=sconv_w, rconv_w=rconv_w, rconv_b=row(rconv_b),
        lru_wa=lru_wa.astype(BF16), lru_ba=row(lru_ba), lru_wx=lru_wx.astype(BF16), lru_bx=row(lru_bx),
        lru_wax=jnp.concatenate([lru_wa, lru_wx], axis=-1).astype(BF16),
        lru_lambda=row(lru_lambda), w_branch=w_branch.astype(BF16), w_merge=w_merge.astype(BF16),
        b_merge=row(b_merge), w_o=w_o.astype(BF16), norm2=row(norm2), w_up=w_up.astype(BF16),
        fconv_w=fconv_w, fconv_b=row(fconv_b), w_down=w_down.astype(BF16), norm_f=norm_f.reshape(1, -1),
    )
    return _trunk(x_prompt, x_sample, cache_k, cache_v, state_sconv, state_rconv, state_lru, state_fconv,
                  page_table, w, TIME_TILE)
```

```python
import functools
import math

import jax
import jax.numpy as jnp
from jax import lax
from jax.experimental import pallas as pl
from jax.experimental.pallas import tpu as pltpu

F32 = jnp.float32
BF16 = jnp.bfloat16

HEAD_DIM = 128
N_HEADS = 8
N_KV_HEADS = 4
GROUP = N_HEADS // N_KV_HEADS
KV_W = N_KV_HEADS * HEAD_DIM
MOBA_BLOCK = 256
MOBA_TOPK = 3
LRU_BLOCKS = 8
LRU_C = 8.0
RMS_EPS = 1e-6
PAGE_SIZE = 128
PAGES_PER_BLOCK = MOBA_BLOCK // PAGE_SIZE
ATT_SCALE = HEAD_DIM ** -0.5
LOG2_E = math.log2(math.e)

SUBLANES = 8
HALO = SUBLANES
VMEM_LIMIT = 56 * 1024 * 1024
POST_VMEM_LIMIT = 60 * 1024 * 1024
TIME_TILE = 256


def _dot(a, b):
    return jnp.dot(a, b, preferred_element_type=F32)


def _dot_nt(a, b, precision=None):
    return lax.dot_general(a, b, (((1,), (1,)), ((), ())), preferred_element_type=F32,
                           precision=precision)


def _rms(x, g):
    return x * lax.rsqrt(jnp.mean(x * x, axis=-1, keepdims=True) + RMS_EPS) * g


def _gelu_tanh(x):
    c = math.sqrt(2.0 / math.pi)
    return x * (0.5 * (1.0 + jnp.tanh(c * (x + 0.044715 * (x * x * x)))))


def _softplus(x):
    return jnp.maximum(x, 0.0) + jnp.log1p(jnp.exp(-jnp.abs(x)))


def _lru_coeffs(xc, r_lin, i_lin, lam):
    r = jax.nn.sigmoid(r_lin)
    i = jax.nn.sigmoid(i_lin)
    log_a = -LRU_C * r * _softplus(-lam)
    a = jnp.exp(log_a)
    th = jnp.tanh(log_a)
    b = jnp.sqrt(-2.0 * th / (1.0 - th)) * (i * xc)
    return a, b


def _const_spec(shape, index):
    return pl.BlockSpec(shape, lambda *_: index, pipeline_mode=pl.Buffered(1))


def _mix_kernel(x_ref, g1_ref, win_ref, scw_ref, rcw_ref, rcb_ref, wax_ref, ba_ref, bx_ref,
                lam_ref, wb1_ref, wb2_ref, wm_ref, bm_ref, kall_ref, vall_ref,
                q_ref, kb_ref, vb_ref, kf_ref, vf_ref, km_ref, g0_ref, mp_ref, scn_ref, rcn_ref, hl_ref,
                ext_sc, ext_rc, lin_a, lin_i, h_carry, *, tm, d):
    del kall_ref, vall_ref
    t = pl.program_id(1)
    last = pl.num_programs(1) - 1

    @pl.when(t == 0)
    def _():
        ext_sc[0:HALO, :] = jnp.zeros((HALO, d), F32)
        ext_rc[0:HALO, :] = jnp.zeros((HALO, d), F32)
        h_carry[...] = jnp.zeros((1, d), F32)

    xn = _rms(x_ref[...], g1_ref[...]).astype(BF16)
    c0 = d + 2 * KV_W
    bw = d // LRU_BLOCKS

    ext_rc[HALO:HALO + tm, :] = _dot(xn, win_ref[:, c0 + 3 * d:c0 + 4 * d])
    u = _dot(xn, win_ref[:, c0 + d:c0 + 2 * d]) * _dot(xn, win_ref[:, c0 + 2 * d:c0 + 3 * d])
    xc = (rcw_ref[0:1, :] * ext_rc[HALO - 3:HALO - 3 + tm, :]
          + rcw_ref[1:2, :] * ext_rc[HALO - 2:HALO - 2 + tm, :]
          + rcw_ref[2:3, :] * ext_rc[HALO - 1:HALO - 1 + tm, :]
          + rcw_ref[3:4, :] * ext_rc[HALO:HALO + tm, :]) + rcb_ref[...]
    xcb = xc.astype(BF16)
    for n in range(LRU_BLOCKS):
        both = _dot(xcb[:, n * bw:(n + 1) * bw], wax_ref[n])
        lin_a[:, n * bw:(n + 1) * bw] = both[:, 0:bw]
        lin_i[:, n * bw:(n + 1) * bw] = both[:, bw:2 * bw]

    b_gate = _dot(xn, win_ref[:, c0:c0 + d])
    qg = GROUP * HEAD_DIM
    for h in range(N_KV_HEADS):
        q_ref[h] = _dot(xn, win_ref[:, h * qg:(h + 1) * qg]).astype(BF16)
    ext_sc[HALO:HALO + tm, :] = u
    sc = (scw_ref[0:1, :] * ext_sc[HALO - 2:HALO - 2 + tm, :]
          + scw_ref[1:2, :] * ext_sc[HALO - 1:HALO - 1 + tm, :]
          + scw_ref[2:3, :] * u)
    o_sc = (b_gate * sc).astype(BF16)

    a, b = _lru_coeffs(xc, lin_a[...] + ba_ref[...], lin_i[...] + bx_ref[...], lam_ref[...])
    lin_a[...] = a
    lin_i[...] = b

    k = _dot(xn, win_ref[:, d:d + KV_W])
    v = _dot(xn, win_ref[:, d + KV_W:d + 2 * KV_W])
    for h in range(N_KV_HEADS):
        kb_ref[h] = k[:, h * HEAD_DIM:(h + 1) * HEAD_DIM].astype(BF16)
        vb_ref[h] = v[:, h * HEAD_DIM:(h + 1) * HEAD_DIM].T.astype(BF16)
        kf_ref[pl.ds(h, tm, stride=N_KV_HEADS), :] = k[:, h * HEAD_DIM:(h + 1) * HEAD_DIM]
        vf_ref[pl.ds(h, tm, stride=N_KV_HEADS), :] = v[:, h * HEAD_DIM:(h + 1) * HEAD_DIM]
    for j in range(tm // MOBA_BLOCK):
        km_ref[j:j + 1, :] = jnp.mean(k[j * MOBA_BLOCK:(j + 1) * MOBA_BLOCK], axis=0, keepdims=True)
    g0_ref[...] = jax.nn.sigmoid(_dot(xn, wm_ref[:, 0:d]) + bm_ref[:, 0:d])

    row = lax.broadcasted_iota(jnp.int32, (SUBLANES, d), 0)
    h_prev = jnp.broadcast_to(h_carry[...], (SUBLANES, d))
    for g in range(tm // SUBLANES):
        rows = slice(g * SUBLANES, (g + 1) * SUBLANES)
        ag = lin_a[rows, :]
        bg = lin_i[rows, :]
        s = 1
        while s < SUBLANES:
            keep = row >= s
            bg = bg + ag * jnp.where(keep, pltpu.roll(bg, s, 0), 0.0)
            ag = ag * jnp.where(keep, pltpu.roll(ag, s, 0), 1.0)
            s *= 2
        hg = ag * h_prev + bg
        lin_i[rows, :] = hg
        h_prev = jnp.broadcast_to(hg[SUBLANES - 1:SUBLANES, :], (SUBLANES, d))
    h_carry[...] = h_prev[0:1, :]

    pb1 = _dot(o_sc, wb1_ref[...])
    g_lru = _dot(xn, win_ref[:, c0 + 4 * d:c0 + 5 * d])
    mp = jax.nn.sigmoid(_dot(xn, wm_ref[:, d:2 * d]) + bm_ref[:, d:2 * d]) * pb1
    gate2 = jax.nn.sigmoid(_dot(xn, wm_ref[:, 2 * d:3 * d]) + bm_ref[:, 2 * d:3 * d])
    o_lru = (lin_i[...] * _gelu_tanh(g_lru)).astype(BF16)
    mp_ref[...] = mp + gate2 * _dot(o_lru, wb2_ref[...])

    @pl.when(t == last)
    def _():
        scn_ref[...] = ext_sc[HALO + tm - 2:HALO + tm, :]
        rcn_ref[...] = ext_rc[HALO + tm - 3:HALO + tm, :]
        hl_ref[...] = h_carry[...]

    ext_sc[HALO - 2:HALO, :] = ext_sc[HALO + tm - 2:HALO + tm, :]
    ext_rc[HALO - 3:HALO, :] = ext_rc[HALO + tm - 3:HALO + tm, :]


def _prompt_mix(l, x, w, k_all, v_all, tm):
    bsz, t, d = x.shape
    nt = t // tm
    nmb = tm // MOBA_BLOCK
    n_in = w['w_in'].shape[-1]
    row = lambda b, i: (b, i, 0)
    per_b = lambda b, i: (b, 0, 0)
    vec = lambda n: _const_spec((None, 1, n), (l, 0, 0))
    in_specs = [
        pl.BlockSpec((None, tm, d), row),
        vec(d),
        _const_spec((None, d, n_in), (l, 0, 0)),
        _const_spec((None, 3, d), (l, 0, 0)),
        _const_spec((None, 4, d), (l, 0, 0)),
        vec(d),
        _const_spec((None, LRU_BLOCKS, d // LRU_BLOCKS, 2 * d // LRU_BLOCKS), (l, 0, 0, 0)),
        vec(d),
        vec(d),
        vec(d),
        _const_spec((None, None, d, d), (l, 1, 0, 0)),
        _const_spec((None, None, d, d), (l, 2, 0, 0)),
        _const_spec((None, d, 3 * d), (l, 0, 0)),
        vec(3 * d),
        pl.BlockSpec(memory_space=pl.ANY),
        pl.BlockSpec(memory_space=pl.ANY),
    ]
    out_shape = [
        jax.ShapeDtypeStruct((bsz, N_KV_HEADS, t, d // N_KV_HEADS), BF16),
        jax.ShapeDtypeStruct((bsz, N_KV_HEADS, t, HEAD_DIM), BF16),
        jax.ShapeDtypeStruct((bsz, N_KV_HEADS, HEAD_DIM, t), BF16),
        jax.ShapeDtypeStruct(k_all.shape, F32),
        jax.ShapeDtypeStruct(v_all.shape, F32),
        jax.ShapeDtypeStruct((bsz, nt, nmb, KV_W), F32),
        jax.ShapeDtypeStruct((bsz, t, d), F32),
        jax.ShapeDtypeStruct((bsz, t, d), F32),
        jax.ShapeDtypeStruct((bsz, 2, d), F32),
        jax.ShapeDtypeStruct((bsz, 3, d), F32),
        jax.ShapeDtypeStruct((bsz, 1, d), F32),
    ]
    head_major = lambda b, i: (b, 0, i, 0)
    out_specs = [
        pl.BlockSpec((None, N_KV_HEADS, tm, d // N_KV_HEADS), head_major),
        pl.BlockSpec((None, N_KV_HEADS, tm, HEAD_DIM), head_major),
        pl.BlockSpec((None, N_KV_HEADS, HEAD_DIM, tm), lambda b, i: (b, 0, 0, i)),
        pl.BlockSpec((None, None, tm * N_KV_HEADS, HEAD_DIM), lambda b, i: (l, b, i, 0)),
        pl.BlockSpec((None, None, tm * N_KV_HEADS, HEAD_DIM), lambda b, i: (l, b, i, 0)),
        pl.BlockSpec((None, None, nmb, KV_W), lambda b, i: (b, i, 0, 0)),
        pl.BlockSpec((None, tm, d), row),
        pl.BlockSpec((None, tm, d), row),
        pl.BlockSpec((None, 2, d), per_b),
        pl.BlockSpec((None, 3, d), per_b),
        pl.BlockSpec((None, 1, d), per_b),
    ]
    scratch = [
        pltpu.VMEM((HALO + tm, d), F32),
        pltpu.VMEM((HALO + tm, d), F32),
        pltpu.VMEM((tm, d), F32),
        pltpu.VMEM((tm, d), F32),
        pltpu.VMEM((1, d), F32),
    ]
    return pl.pallas_call(
        functools.partial(_mix_kernel, tm=tm, d=d),
        grid=(bsz, nt), in_specs=in_specs, out_specs=out_specs, out_shape=out_shape,
        scratch_shapes=scratch, name=f"prompt_mix_{l}",
        input_output_aliases={len(in_specs) - 2: 3, len(in_specs) - 1: 4},
        compiler_params=pltpu.CompilerParams(
            dimension_semantics=("arbitrary", "arbitrary"), vmem_limit_bytes=VMEM_LIMIT),
    )(x, w['norm1'], w['w_in'], w['sconv_w'], w['rconv_w'], w['rconv_b'], w['lru_wax'], w['lru_ba'],
      w['lru_bx'], w['lru_lambda'], w['w_branch'], w['w_branch'], w['w_merge'], w['b_merge'], k_all, v_all)


def _attn_kernel(q_ref, k_ref, v_ref, km_ref, o_ref, *, nb):
    blk = MOBA_BLOCK
    nq = GROUP * blk

    parts = []
    rest = km_ref[...]
    for _ in range(3):
        part = rest.astype(BF16).astype(F32)
        parts += [part, jnp.zeros_like(part)]
        rest = rest - part
    km_terms = jnp.concatenate(parts, axis=0).astype(BF16)
    prow = 2 * nb

    block_id = lax.broadcasted_iota(jnp.int32, (nb, nq), 0)
    key = lax.broadcasted_iota(jnp.int32, (blk, nq), 0)
    qpos = lax.broadcasted_iota(jnp.int32, (blk, nq), 1) & (blk - 1)
    causal = key <= qpos

    def scores(i):
        q2 = jnp.concatenate([q_ref[i * blk:(i + 1) * blk, g * HEAD_DIM:(g + 1) * HEAD_DIM]
                              for g in range(GROUP)], axis=0)
        keys = k_ref[0:(i + 1) * blk, :]
        return _dot_nt(keys if i == 0 else jnp.concatenate([keys, km_terms], axis=0), q2)

    s_next = scores(0)
    for i in range(nb):
        q_rows = slice(i * blk, (i + 1) * blk)
        n_keys = (i + 1) * blk
        s_all = s_next
        if i + 1 < nb:
            s_next = scores(i + 1)
        if i > 0:
            gate = (s_all[n_keys + 2 * prow:n_keys + 2 * prow + nb]
                    + s_all[n_keys + prow:n_keys + prow + nb] + s_all[n_keys:n_keys + nb])
            past = block_id < i
            gate = jnp.where(past, gate, -jnp.inf)
            rank = jnp.zeros((nb, nq), jnp.int32)
            for j in range(i):
                gj = gate[j:j + 1, :]
                rank = rank + jnp.where((gj > gate) | ((gj == gate) & (j < block_id)), 1, 0)
            sel = jnp.where(past & (rank < MOBA_TOPK), 1.0, 0.0)

        pieces = [jnp.where(sel[j:j + 1, :] > 0.5, s_all[j * blk:(j + 1) * blk], -jnp.inf) for j in range(i)]
        pieces.append(jnp.where(causal, s_all[i * blk:n_keys], -jnp.inf))
        m = jnp.max(pieces[-1], axis=0, keepdims=True)
        for piece in pieces[:-1]:
            m = jnp.maximum(m, jnp.max(piece, axis=0, keepdims=True))
        probs = [jnp.exp2((piece - m) * (ATT_SCALE * LOG2_E)) for piece in pieces]
        den = jnp.sum(probs[0], axis=0, keepdims=True)
        for pr in probs[1:]:
            den = den + jnp.sum(pr, axis=0, keepdims=True)
        p_all = jnp.concatenate([pr.astype(BF16) for pr in probs], axis=0)
        out = (_dot(v_ref[:, 0:n_keys], p_all) / den).T
        for g in range(GROUP):
            o_ref[q_rows, g * HEAD_DIM:(g + 1) * HEAD_DIM] = out[g * blk:(g + 1) * blk].astype(o_ref.dtype)


def _prompt_attn(q, k, v, kmean):
    bsz, _, t, gw = q.shape
    nb = t // MOBA_BLOCK
    return pl.pallas_call(
        functools.partial(_attn_kernel, nb=nb),
        grid=(bsz, N_KV_HEADS),
        in_specs=[
            pl.BlockSpec((None, None, t, gw), lambda b, h: (b, h, 0, 0)),
            pl.BlockSpec((None, None, t, HEAD_DIM), lambda b, h: (b, h, 0, 0)),
            pl.BlockSpec((None, None, HEAD_DIM, t), lambda b, h: (b, h, 0, 0)),
            pl.BlockSpec((None, nb, HEAD_DIM), lambda b, h: (b, 0, h)),
        ],
        out_specs=pl.BlockSpec((None, None, t, gw), lambda b, h: (b, h, 0, 0)),
        out_shape=jax.ShapeDtypeStruct((bsz, N_KV_HEADS, t, gw), BF16),
        name="prompt_attn",
        compiler_params=pltpu.CompilerParams(
            dimension_semantics=("arbitrary", "arbitrary"), vmem_limit_bytes=VMEM_LIMIT),
    )(q, k, v, kmean)


def _post_kernel(pt_ref, x_ref, att_ref, g0_ref, mp_ref, wb0_ref, wo_ref, g2_ref, wup_ref, fcw_ref, fcb_ref,
                 wdn_ref, gf_ref, cache_ref, y_ref, fcn_ref, km_ref, ext_fc, cbuf, csem,
                 *, layer, tm, d, dff, final, pps):
    t = pl.program_id(1)
    last = pl.num_programs(1) - 1
    step = pl.program_id(0) * pl.num_programs(1) + t
    n_steps = pl.num_programs(0) * pl.num_programs(1)
    slot = step % 2

    def page_copies(s, sl):
        return [pltpu.make_async_copy(cache_ref.at[layer, pt_ref[s * pps + p]], cbuf.at[sl, p], csem.at[sl, p])
                for p in range(pps)]

    @pl.when(step == 0)
    def _():
        for cp in page_copies(step, 0):
            cp.start()

    for cp in page_copies(step, slot):
        cp.wait()

    att = jnp.concatenate([att_ref[h] for h in range(N_KV_HEADS)], axis=1)
    merged = g0_ref[...] * _dot(att, wb0_ref[...]) + mp_ref[...]
    x1 = x_ref[...] + _dot(merged.astype(BF16), wo_ref[...])

    for cp in page_copies(jnp.minimum(step + 1, n_steps - 1), 1 - slot):
        cp.start()
    groups = PAGE_SIZE * N_KV_HEADS // SUBLANES
    for j in range(pps // PAGES_PER_BLOCK):
        tot = jnp.zeros((SUBLANES, HEAD_DIM), F32)
        for p in range(PAGES_PER_BLOCK):
            page = cbuf.at[slot, PAGES_PER_BLOCK * j + p]
            for g in range(groups):
                tot = tot + page[g * SUBLANES:(g + 1) * SUBLANES, :]
        tot = (tot[0:N_KV_HEADS] + tot[N_KV_HEADS:2 * N_KV_HEADS]) * (1.0 / MOBA_BLOCK)
        for h in range(N_KV_HEADS):
            km_ref[h, j:j + 1, :] = tot[h:h + 1, :]

    @pl.when(t == 0)
    def _():
        ext_fc[0:HALO, :] = jnp.zeros((HALO, 2 * dff), F32)

    xn2 = _rms(x1, g2_ref[...]).astype(BF16)
    ext_fc[HALO:HALO + tm, :] = _dot(xn2, wup_ref[...])

    def conv(c0, c1):
        return (fcw_ref[0:1, c0:c1] * ext_fc[HALO - 2:HALO - 2 + tm, c0:c1]
                + fcw_ref[1:2, c0:c1] * ext_fc[HALO - 1:HALO - 1 + tm, c0:c1]
                + fcw_ref[2:3, c0:c1] * ext_fc[HALO:HALO + tm, c0:c1]) + fcb_ref[:, c0:c1]

    hh = (jax.nn.silu(conv(0, dff)) * conv(dff, 2 * dff)).astype(BF16)
    x2 = x1 + _dot(hh, wdn_ref[...])
    y_ref[...] = _rms(x2, gf_ref[...]) if final else x2

    @pl.when(t == last)
    def _():
        fcn_ref[...] = ext_fc[HALO + tm - 2:HALO + tm, :]

    ext_fc[HALO - 2:HALO, :] = ext_fc[HALO + tm - 2:HALO + tm, :]

    @pl.when(step == n_steps - 1)
    def _():
        for cp in page_copies(step, 1 - slot):
            cp.wait()


def _prompt_post(l, x, att, g0, mp, w, cache_k, page_table, tm, final):
    bsz, t, d = x.shape
    dff = w['w_down'].shape[1]
    depth, n_pool, page, hkv, hd = cache_k.shape
    n_b, n_pages = page_table.shape
    n_steps = bsz * (t // tm)
    pps = n_b * n_pages // n_steps
    assert pps * n_steps == n_b * n_pages and n_pages % pps == 0
    assert pps % (PAGES_PER_BLOCK * SUBLANES) == 0
    bps = pps // PAGES_PER_BLOCK
    nt = t // tm
    row = lambda b, i, pt: (b, i, 0)
    tile = pl.BlockSpec((None, tm, d), row)
    in_specs = [
        tile, pl.BlockSpec((None, N_KV_HEADS, tm, d // N_KV_HEADS), lambda b, i, pt: (b, 0, i, 0)), tile, tile,
        _const_spec((None, None, d, d), (l, 0, 0, 0)),
        _const_spec((None, d, d), (l, 0, 0)),
        _const_spec((None, 1, d), (l, 0, 0)),
        _const_spec((None, d, 2 * dff), (l, 0, 0)),
        _const_spec((None, 3, 2 * dff), (l, 0, 0)),
        _const_spec((None, 1, 2 * dff), (l, 0, 0)),
        _const_spec((None, dff, d), (l, 0, 0)),
        _const_spec((1, d), (0, 0)),
        pl.BlockSpec(memory_space=pl.ANY),
    ]
    grid_spec = pltpu.PrefetchScalarGridSpec(
        num_scalar_prefetch=1,
        grid=(bsz, nt),
        in_specs=in_specs,
        out_specs=[tile,
                   pl.BlockSpec((None, 2, 2 * dff), lambda b, i, pt: (b, 0, 0)),
                   pl.BlockSpec((None, hkv, bps, hd), lambda b, i, pt: (b * nt + i, 0, 0, 0))],
        scratch_shapes=[pltpu.VMEM((HALO + tm, 2 * dff), F32),
                        pltpu.VMEM((2, pps, page * hkv, hd), F32),
                        pltpu.SemaphoreType.DMA((2, pps))],
    )
    y, fcn, km = pl.pallas_call(
        functools.partial(_post_kernel, layer=l, tm=tm, d=d, dff=dff, final=final, pps=pps),
        grid_spec=grid_spec,
        out_shape=[jax.ShapeDtypeStruct((bsz, t, d), F32),
                   jax.ShapeDtypeStruct((bsz, 2, 2 * dff), F32),
                   jax.ShapeDtypeStruct((n_steps, hkv, bps, hd), F32)],
        name=f"prompt_post_{l}",
        compiler_params=pltpu.CompilerParams(
            dimension_semantics=("arbitrary", "arbitrary"), vmem_limit_bytes=POST_VMEM_LIMIT),
    )(page_table.reshape(-1), x, att, g0, mp, w['w_branch'], w['w_o'], w['norm2'], w['w_up'], w['fconv_w'],
      w['fconv_b'], w['w_down'], w['norm_f'], cache_k.reshape(depth, n_pool, page * hkv, hd))
    return y, fcn, km.reshape(n_b, n_pages // pps, hkv, bps, hd)


def _smix_kernel(x_ref, g1_ref, win_ref, scw_ref, rcw_ref, rcb_ref, wa_ref, ba_ref, wx_ref, bx_ref,
                 lam_ref, wb1_ref, wb2_ref, wm_ref, bm_ref, sst_ref, rst_ref, h0_ref,
                 q_ref, k_ref, v_ref, g0_ref, mp_ref, scn_ref, rcn_ref, hl_ref, *, d):
    xn = _rms(x_ref[...], g1_ref[...]).astype(BF16)
    q_ref[...] = _dot(xn, win_ref[:, 0:d])
    k_ref[...] = _dot(xn, win_ref[:, d:d + KV_W])
    v_ref[...] = _dot(xn, win_ref[:, d + KV_W:d + 2 * KV_W])
    c0 = d + 2 * KV_W

    u = _dot(xn, win_ref[:, c0 + d:c0 + 2 * d]) * _dot(xn, win_ref[:, c0 + 2 * d:c0 + 3 * d])
    sc = scw_ref[0:1, :] * sst_ref[0] + scw_ref[1:2, :] * sst_ref[1] + scw_ref[2:3, :] * u
    scn_ref[0] = sst_ref[1]
    scn_ref[1] = u
    o_sc = (_dot(xn, win_ref[:, c0:c0 + d]) * sc).astype(BF16)
    pb1 = _dot(o_sc, wb1_ref[...])

    x_lru = _dot(xn, win_ref[:, c0 + 3 * d:c0 + 4 * d])
    xc = (rcw_ref[0:1, :] * rst_ref[0] + rcw_ref[1:2, :] * rst_ref[1] + rcw_ref[2:3, :] * rst_ref[2]
          + rcw_ref[3:4, :] * x_lru) + rcb_ref[...]
    rcn_ref[0] = rst_ref[1]
    rcn_ref[1] = rst_ref[2]
    rcn_ref[2] = x_lru
    bw = d // LRU_BLOCKS
    xcb = xc.astype(BF16)
    r_lin = jnp.concatenate(
        [_dot(xcb[:, n * bw:(n + 1) * bw], wa_ref[n]) for n in range(LRU_BLOCKS)], axis=1)
    i_lin = jnp.concatenate(
        [_dot(xcb[:, n * bw:(n + 1) * bw], wx_ref[n]) for n in range(LRU_BLOCKS)], axis=1)
    a, b = _lru_coeffs(xc, r_lin + ba_ref[...], i_lin + bx_ref[...], lam_ref[...])
    h = a * h0_ref[...] + b
    hl_ref[...] = h
    o_lru = (h * _gelu_tanh(_dot(xn, win_ref[:, c0 + 4 * d:c0 + 5 * d]))).astype(BF16)
    pb2 = _dot(o_lru, wb2_ref[...])

    g0_ref[...] = jax.nn.sigmoid(_dot(xn, wm_ref[:, 0:d]) + bm_ref[:, 0:d])
    mp_ref[...] = (jax.nn.sigmoid(_dot(xn, wm_ref[:, d:2 * d]) + bm_ref[:, d:2 * d]) * pb1
                   + jax.nn.sigmoid(_dot(xn, wm_ref[:, 2 * d:3 * d]) + bm_ref[:, 2 * d:3 * d]) * pb2)


def _sample_mix(l, x, sst, rst, h0, w):
    n, d = x.shape
    n_in = w['w_in'].shape[-1]
    full = lambda shape: pl.BlockSpec(shape, lambda i: (0,) * len(shape))
    lay = lambda shape, idx: pl.BlockSpec(shape, lambda i: idx)
    vec = lambda m: lay((None, 1, m), (l, 0, 0))
    lw = (None, LRU_BLOCKS, d // LRU_BLOCKS, d // LRU_BLOCKS)
    in_specs = [
        full((n, d)), vec(d), lay((None, d, n_in), (l, 0, 0)), lay((None, 3, d), (l, 0, 0)),
        lay((None, 4, d), (l, 0, 0)), vec(d), lay(lw, (l, 0, 0, 0)), vec(d), lay(lw, (l, 0, 0, 0)),
        vec(d), vec(d), lay((None, None, d, d), (l, 1, 0, 0)), lay((None, None, d, d), (l, 2, 0, 0)),
        lay((None, d, 3 * d), (l, 0, 0)), vec(3 * d),
        lay((None, 2, n, d), (l, 0, 0, 0)), lay((None, 3, n, d), (l, 0, 0, 0)), lay((None, n, d), (l, 0, 0)),
    ]
    out_shape = [
        jax.ShapeDtypeStruct((n, d), F32), jax.ShapeDtypeStruct((n, KV_W), F32),
        jax.ShapeDtypeStruct((n, KV_W), F32), jax.ShapeDtypeStruct((n, d), F32),
        jax.ShapeDtypeStruct((n, d), F32), jax.ShapeDtypeStruct((2, n, d), F32),
        jax.ShapeDtypeStruct((3, n, d), F32), jax.ShapeDtypeStruct((n, d), F32),
    ]
    out_specs = [full(s.shape) for s in out_shape]
    return pl.pallas_call(
        functools.partial(_smix_kernel, d=d),
        grid=(1,), in_specs=in_specs, out_specs=out_specs, out_shape=out_shape,
        name=f"sample_mix_{l}",
        compiler_params=pltpu.CompilerParams(
            dimension_semantics=("arbitrary",), vmem_limit_bytes=VMEM_LIMIT),
    )(x, w['norm1'], w['w_in'], w['sconv_w'], w['rconv_w'], w['rconv_b'], w['lru_wa'], w['lru_ba'],
      w['lru_wx'], w['lru_bx'], w['lru_lambda'], w['w_branch'], w['w_branch'], w['w_merge'],
      w['b_merge'], sst, rst, h0)


def _spost_kernel(x_ref, att_ref, g0_ref, mp_ref, wb0_ref, wo_ref, g2_ref, wup_ref, fcw_ref, fcb_ref,
                  wdn_ref, gf_ref, fst_ref, y_ref, fcn_ref, *, dff, final):
    merged = g0_ref[...] * _dot(att_ref[...].astype(BF16), wb0_ref[...]) + mp_ref[...]
    x1 = x_ref[...] + _dot(merged.astype(BF16), wo_ref[...])
    xn2 = _rms(x1, g2_ref[...]).astype(BF16)
    up = _dot(xn2, wup_ref[...])
    c = (fcw_ref[0:1, :] * fst_ref[0] + fcw_ref[1:2, :] * fst_ref[1] + fcw_ref[2:3, :] * up) + fcb_ref[...]
    fcn_ref[0] = fst_ref[1]
    fcn_ref[1] = up
    hh = (jax.nn.silu(c[:, 0:dff]) * c[:, dff:2 * dff]).astype(BF16)
    x2 = x1 + _dot(hh, wdn_ref[...])
    y_ref[...] = _rms(x2, gf_ref[...]) if final else x2


def _sample_post(l, x, att, g0, mp, fst, w, final):
    n, d = x.shape
    dff = w['w_down'].shape[1]
    full = lambda shape: pl.BlockSpec(shape, lambda i: (0,) * len(shape))
    lay = lambda shape, idx: pl.BlockSpec(shape, lambda i: idx)
    in_specs = [
        full((n, d)), full((n, d)), full((n, d)), full((n, d)),
        lay((None, None, d, d), (l, 0, 0, 0)), lay((None, d, d), (l, 0, 0)), lay((None, 1, d), (l, 0, 0)),
        lay((None, d, 2 * dff), (l, 0, 0)), lay((None, 3, 2 * dff), (l, 0, 0)),
        lay((None, 1, 2 * dff), (l, 0, 0)), lay((None, dff, d), (l, 0, 0)), full((1, d)),
        lay((None, 2, n, 2 * dff), (l, 0, 0, 0)),
    ]
    out_shape = [jax.ShapeDtypeStruct((n, d), F32), jax.ShapeDtypeStruct((2, n, 2 * dff), F32)]
    return pl.pallas_call(
        functools.partial(_spost_kernel, dff=dff, final=final),
        grid=(1,), in_specs=in_specs, out_specs=[full(s.shape) for s in out_shape], out_shape=out_shape,
        name=f"sample_post_{l}",
        compiler_params=pltpu.CompilerParams(
            dimension_semantics=("arbitrary",), vmem_limit_bytes=VMEM_LIMIT),
    )(x, att, g0, mp, w['w_branch'], w['w_o'], w['norm2'], w['w_up'], w['fconv_w'], w['fconv_b'],
      w['w_down'], w['norm_f'], fst)


CHOOSE_SEQS = 8


def _choose_kernel(q_ref, km_ref, o_ref, *, nb):
    lane = lax.broadcasted_iota(jnp.int32, (nb, HEAD_DIM), 1)
    blk = lax.broadcasted_iota(jnp.int32, (nb, HEAD_DIM), 0)
    row = lax.broadcasted_iota(jnp.int32, (SUBLANES, HEAD_DIM), 0)
    for s in range(q_ref.shape[0]):
        gate = jnp.zeros((nb, HEAD_DIM), F32)
        for h in range(N_HEADS):
            gh = jnp.concatenate(
                [jnp.sum(km_ref[s, part, h // GROUP] * q_ref[s, h:h + 1, :], axis=1, keepdims=True)
                 for part in range(km_ref.shape[1])], axis=0)
            gate = jnp.where(lane == h, gh, gate)
        out = jnp.zeros((SUBLANES, HEAD_DIM), jnp.int32)
        for r in range(MOBA_TOPK):
            best = jnp.max(gate, axis=0, keepdims=True)
            idx = jnp.min(jnp.where(gate == best, blk, nb), axis=0, keepdims=True)
            out = jnp.where(row == r, idx, out)
            gate = jnp.where(blk == idx, -jnp.inf, gate)
        o_ref[s] = out


def _sample_choose(q, kmean_l):
    n = q.shape[0]
    parts, bps = kmean_l.shape[1], kmean_l.shape[3]
    ns = math.gcd(n, CHOOSE_SEQS)
    return pl.pallas_call(
        functools.partial(_choose_kernel, nb=parts * bps),
        grid=(n // ns,),
        in_specs=[pl.BlockSpec((ns, N_HEADS, HEAD_DIM), lambda b: (b, 0, 0)),
                  pl.BlockSpec((ns, parts, N_KV_HEADS, bps, HEAD_DIM), lambda b: (b, 0, 0, 0, 0))],
        out_specs=pl.BlockSpec((ns, SUBLANES, HEAD_DIM), lambda b: (b, 0, 0)),
        out_shape=jax.ShapeDtypeStruct((n, SUBLANES, HEAD_DIM), jnp.int32),
        name="sample_choose",
        compiler_params=pltpu.CompilerParams(dimension_semantics=("arbitrary",)),
    )(q.reshape(n, N_HEADS, HEAD_DIM), kmean_l)


def _sattn_kernel(sel_ref, pt_ref, q_ref, kn_ref, vn_ref, ck_ref, cv_ref, o_ref, kbuf, vbuf, sem,
                  *, l, n_pages):
    b = pl.program_id(0)
    n = pl.num_programs(0)
    slot = b % 2

    def copies(seq, sl):
        out = []
        for h in range(N_HEADS):
            kv = h // GROUP
            for r in range(MOBA_TOPK):
                blk = sel_ref[(seq * MOBA_TOPK + r) * N_HEADS + h]
                for p in range(PAGES_PER_BLOCK):
                    page = pt_ref[seq * n_pages + blk * PAGES_PER_BLOCK + p]
                    part = r * PAGES_PER_BLOCK + p
                    for c, (src, dst) in enumerate(((ck_ref, kbuf), (cv_ref, vbuf))):
                        out.append(pltpu.make_async_copy(
                            src.at[l, page, :, kv, :],
                            dst.at[sl, h, pl.ds(part * PAGE_SIZE, PAGE_SIZE), :],
                            sem.at[sl, c, h, part]))
        return out

    @pl.when(b == 0)
    def _():
        for cp in copies(b, 0):
            cp.start()

    for cp in copies(b, slot):
        cp.wait()
    for cp in copies(jnp.minimum(b + 1, n - 1), 1 - slot):
        cp.start()

    for h in range(N_HEADS):
        kv = h // GROUP
        q = q_ref[h:h + 1, :]
        s = jnp.sum(kbuf[slot, h] * q, axis=1, keepdims=True) * ATT_SCALE
        s_new = jnp.sum(kn_ref[kv:kv + 1, :] * q, axis=1, keepdims=True) * ATT_SCALE
        m = jnp.maximum(jnp.max(s, axis=0, keepdims=True), s_new)
        p = jnp.exp(s - m)
        p_new = jnp.exp(s_new - m)
        den = jnp.sum(p, axis=0, keepdims=True) + p_new
        num = jnp.sum(p * vbuf[slot, h], axis=0, keepdims=True) + p_new * vn_ref[kv:kv + 1, :]
        o_ref[h:h + 1, :] = num / den

    @pl.when(b == n - 1)
    def _():
        for cp in copies(b, 1 - slot):
            cp.wait()


def _sample_attn(l, sel, page_table, q, k_new, v_new, cache_k, cache_v):
    n = q.shape[0]
    n_pages = page_table.shape[1]
    rows = MOBA_TOPK * MOBA_BLOCK
    grid_spec = pltpu.PrefetchScalarGridSpec(
        num_scalar_prefetch=2,
        grid=(n,),
        in_specs=[pl.BlockSpec((None, N_HEADS, HEAD_DIM), lambda b, *_: (b, 0, 0)),
                  pl.BlockSpec((None, N_KV_HEADS, HEAD_DIM), lambda b, *_: (b, 0, 0)),
                  pl.BlockSpec((None, N_KV_HEADS, HEAD_DIM), lambda b, *_: (b, 0, 0)),
                  pl.BlockSpec(memory_space=pl.ANY),
                  pl.BlockSpec(memory_space=pl.ANY)],
        out_specs=pl.BlockSpec((None, N_HEADS, HEAD_DIM), lambda b, *_: (b, 0, 0)),
        scratch_shapes=[pltpu.VMEM((2, N_HEADS, rows, HEAD_DIM), F32),
                        pltpu.VMEM((2, N_HEADS, rows, HEAD_DIM), F32),
                        pltpu.SemaphoreType.DMA((2, 2, N_HEADS, MOBA_TOPK * PAGES_PER_BLOCK))],
    )
    sel_flat = sel[:, :MOBA_TOPK, :N_HEADS].reshape(-1)
    out = pl.pallas_call(
        functools.partial(_sattn_kernel, l=l, n_pages=n_pages),
        grid_spec=grid_spec,
        out_shape=jax.ShapeDtypeStruct((n, N_HEADS, HEAD_DIM), F32),
        name=f"sample_attn_{l}",
        compiler_params=pltpu.CompilerParams(dimension_semantics=("arbitrary",)),
    )(sel_flat, page_table.reshape(-1), q.reshape(n, N_HEADS, HEAD_DIM),
      k_new.reshape(n, N_KV_HEADS, HEAD_DIM), v_new.reshape(n, N_KV_HEADS, HEAD_DIM), cache_k, cache_v)
    return out.reshape(n, N_HEADS * HEAD_DIM)


def _trunk(x_prompt, x_sample, cache_k, cache_v, state_sconv, state_rconv, state_lru, state_fconv,
           page_table, w, tm):
    depth = w['w_in'].shape[0]
    bsz, t, d = x_prompt.shape
    n = x_sample.shape[0]
    xp = x_prompt
    xs = x_sample.reshape(n, d)
    sst = jnp.swapaxes(state_sconv, 1, 2)
    rst = jnp.swapaxes(state_rconv, 1, 2)
    fst = jnp.swapaxes(state_fconv, 1, 2)

    k_all = jnp.zeros((depth, bsz, t * N_KV_HEADS, HEAD_DIM), F32)
    v_all = jnp.zeros((depth, bsz, t * N_KV_HEADS, HEAD_DIM), F32)
    st_p, st_s = [], []
    for l in range(depth):
        final = l == depth - 1
        q, kb, vb, k_all, v_all, km, g0, mp, scn, rcn, hl = _prompt_mix(l, xp, w, k_all, v_all, tm)
        att = _prompt_attn(q, kb, vb, km.reshape(bsz, t // MOBA_BLOCK, KV_W))
        xp, fcn, kmean_s = _prompt_post(l, xp, att, g0, mp, w, cache_k, page_table, tm, final)
        st_p.append((scn, rcn, hl.reshape(bsz, d), fcn))

        qs, ks, vs, g0s, mps, scns, rcns, hls = _sample_mix(l, xs, sst, rst, state_lru, w)
        sel = _sample_choose(qs, kmean_s)
        atts = _sample_attn(l, sel, page_table, qs, ks, vs, cache_k, cache_v)
        xs, fcns = _sample_post(l, xs, atts, g0s, mps, fst, w, final)
        st_s.append((ks.reshape(n, 1, N_KV_HEADS, HEAD_DIM), vs.reshape(n, 1, N_KV_HEADS, HEAD_DIM),
                     jnp.swapaxes(scns, 0, 1), jnp.swapaxes(rcns, 0, 1), hls, jnp.swapaxes(fcns, 0, 1)))

    outs_p = [k_all.reshape(depth, bsz, t, N_KV_HEADS, HEAD_DIM), v_all.reshape(depth, bsz, t, N_KV_HEADS, HEAD_DIM)]
    outs_p += [jnp.stack([s[i] for s in st_p], axis=0) for i in range(4)]
    outs_s = [jnp.stack([s[i] for s in st_s], axis=0) for i in range(6)]
    return (xp, xs.reshape(n, 1, d), *outs_p, *outs_s)


def kernel(x_prompt, x_sample, cache_k, cache_v, state_sconv, state_rconv, state_lru, state_fconv, page_table, norm1, w_in, sconv_w, rconv_w, rconv_b, lru_wa, lru_ba, lru_wx, lru_bx, lru_lambda, w_branch, w_merge, b_merge, w_o, norm2, w_up, fconv_w, fconv_b, w_down, norm_f):
    row = lambda a: a.reshape(a.shape[0], 1, a.shape[1])
    w = dict(
        norm1=row(norm1), w_in=w_in.astype(BF16), sconv_w=sconv_w, rconv_w=rconv_w, rconv_b=row(rconv_b),
        lru_wa=lru_wa.astype(BF16), lru_ba=row(lru_ba), lru_wx=lru_wx.astype(BF16), lru_bx=row(lru_bx),
        lru_wax=jnp.concatenate([lru_wa, lru_wx], axis=-1).astype(BF16),
        lru_lambda=row(lru_lambda), w_branch=w_branch.astype(BF16), w_merge=w_merge.astype(BF16),
        b_merge=row(b_merge), w_o=w_o.astype(BF16), norm2=row(norm2), w_up=w_up.astype(BF16),
        fconv_w=fconv_w, fconv_b=row(fconv_b), w_down=w_down.astype(BF16), norm_f=norm_f.reshape(1, -1),
    )
    return _trunk(x_prompt, x_sample, cache_k, cache_v, state_sconv, state_rconv, state_lru, state_fconv,
                  page_table, w, TIME_TILE)
```

```python
import functools
import math

import jax
import jax.numpy as jnp
from jax import lax
from jax.experimental import pallas as pl
from jax.experimental.pallas import tpu as pltpu

F32 = jnp.float32
BF16 = jnp.bfloat16

HEAD_DIM = 128
N_HEADS = 8
N_KV_HEADS = 4
GROUP = N_HEADS // N_KV_HEADS
KV_W = N_KV_HEADS * HEAD_DIM
MOBA_BLOCK = 256
MOBA_TOPK = 3
LRU_BLOCKS = 8
LRU_C = 8.0
RMS_EPS = 1e-6
PAGE_SIZE = 128
PAGES_PER_BLOCK = MOBA_BLOCK // PAGE_SIZE
ATT_SCALE = HEAD_DIM ** -0.5
LOG2_E = math.log2(math.e)

SUBLANES = 8
HALO = SUBLANES
VMEM_LIMIT = 56 * 1024 * 1024
POST_VMEM_LIMIT = 60 * 1024 * 1024
TIME_TILE = 256


def _dot(a, b):
    return jnp.dot(a, b, preferred_element_type=F32)


def _dot_nt(a, b, precision=None):
    return lax.dot_general(a, b, (((1,), (1,)), ((), ())), preferred_element_type=F32,
                           precision=precision)


def _rms(x, g):
    return x * lax.rsqrt(jnp.mean(x * x, axis=-1, keepdims=True) + RMS_EPS) * g


def _gelu_tanh(x):
    c = math.sqrt(2.0 / math.pi)
    return x * (0.5 * (1.0 + jnp.tanh(c * (x + 0.044715 * (x * x * x)))))


def _softplus(x):
    return jnp.maximum(x, 0.0) + jnp.log1p(jnp.exp(-jnp.abs(x)))


def _lru_coeffs(xc, r_lin, i_lin, lam):
    r = jax.nn.sigmoid(r_lin)
    i = jax.nn.sigmoid(i_lin)
    log_a = -LRU_C * r * _softplus(-lam)
    a = jnp.exp(log_a)
    th = jnp.tanh(log_a)
    b = jnp.sqrt(-2.0 * th / (1.0 - th)) * (i * xc)
    return a, b


def _const_spec(shape, index):
    return pl.BlockSpec(shape, lambda *_: index, pipeline_mode=pl.Buffered(1))


def _mix_kernel(x_ref, g1_ref, win_ref, scw_ref, rcw_ref, rcb_ref, wax_ref, ba_ref, bx_ref,
                lam_ref, wb1_ref, wb2_ref, wm_ref, bm_ref, kall_ref, vall_ref,
                q_ref, kb_ref, vb_ref, kf_ref, vf_ref, km_ref, g0_ref, mp_ref, scn_ref, rcn_ref, hl_ref,
                ext_sc, ext_rc, lin_a, lin_i, h_carry, *, tm, d):
    del kall_ref, vall_ref
    t = pl.program_id(1)
    last = pl.num_programs(1) - 1

    @pl.when(t == 0)
    def _():
        ext_sc[0:HALO, :] = jnp.zeros((HALO, d), F32)
        ext_rc[0:HALO, :] = jnp.zeros((HALO, d), F32)
        h_carry[...] = jnp.zeros((1, d), F32)

    xn = _rms(x_ref[...], g1_ref[...]).astype(BF16)
    c0 = d + 2 * KV_W
    bw = d // LRU_BLOCKS

    ext_rc[HALO:HALO + tm, :] = _dot(xn, win_ref[:, c0 + 3 * d:c0 + 4 * d])
    u = _dot(xn, win_ref[:, c0 + d:c0 + 2 * d]) * _dot(xn, win_ref[:, c0 + 2 * d:c0 + 3 * d])
    xc = (rcw_ref[0:1, :] * ext_rc[HALO - 3:HALO - 3 + tm, :]
          + rcw_ref[1:2, :] * ext_rc[HALO - 2:HALO - 2 + tm, :]
          + rcw_ref[2:3, :] * ext_rc[HALO - 1:HALO - 1 + tm, :]
          + rcw_ref[3:4, :] * ext_rc[HALO:HALO + tm, :]) + rcb_ref[...]
    xcb = xc.astype(BF16)
    for n in range(LRU_BLOCKS):
        both = _dot(xcb[:, n * bw:(n + 1) * bw], wax_ref[n])
        lin_a[:, n * bw:(n + 1) * bw] = both[:, 0:bw]
        lin_i[:, n * bw:(n + 1) * bw] = both[:, bw:2 * bw]

    b_gate = _dot(xn, win_ref[:, c0:c0 + d])
    qg = GROUP * HEAD_DIM
    for h in range(N_KV_HEADS):
        q_ref[h] = _dot(xn, win_ref[:, h * qg:(h + 1) * qg]).astype(BF16)
    ext_sc[HALO:HALO + tm, :] = u
    sc = (scw_ref[0:1, :] * ext_sc[HALO - 2:HALO - 2 + tm, :]
          + scw_ref[1:2, :] * ext_sc[HALO - 1:HALO - 1 + tm, :]
          + scw_ref[2:3, :] * u)
    o_sc = (b_gate * sc).astype(BF16)

    a, b = _lru_coeffs(xc, lin_a[...] + ba_ref[...], lin_i[...] + bx_ref[...], lam_ref[...])
    lin_a[...] = a
    lin_i[...] = b

    k = _dot(xn, win_ref[:, d:d + KV_W])
    v = _dot(xn, win_ref[:, d + KV_W:d + 2 * KV_W])
    for h in range(N_KV_HEADS):
        kb_ref[h] = k[:, h * HEAD_DIM:(h + 1) * HEAD_DIM].astype(BF16)
        vb_ref[h] = v[:, h * HEAD_DIM:(h + 1) * HEAD_DIM].T.astype(BF16)
        kf_ref[pl.ds(h, tm, stride=N_KV_HEADS), :] = k[:, h * HEAD_DIM:(h + 1) * HEAD_DIM]
        vf_ref[pl.ds(h, tm, stride=N_KV_HEADS), :] = v[:, h * HEAD_DIM:(h + 1) * HEAD_DIM]
    for j in range(tm // MOBA_BLOCK):
        km_ref[j:j + 1, :] = jnp.mean(k[j * MOBA_BLOCK:(j + 1) * MOBA_BLOCK], axis=0, keepdims=True)
    g0_ref[...] = jax.nn.sigmoid(_dot(xn, wm_ref[:, 0:d]) + bm_ref[:, 0:d])

    row = lax.broadcasted_iota(jnp.int32, (SUBLANES, d), 0)
    h_prev = jnp.broadcast_to(h_carry[...], (SUBLANES, d))
    for g in range(tm // SUBLANES):
        rows = slice(g * SUBLANES, (g + 1) * SUBLANES)
        ag = lin_a[rows, :]
        bg = lin_i[rows, :]
        s = 1
        while s < SUBLANES:
            keep = row >= s
            bg = bg + ag * jnp.where(keep, pltpu.roll(bg, s, 0), 0.0)
            ag = ag * jnp.where(keep, pltpu.roll(ag, s, 0), 1.0)
            s *= 2
        hg = ag * h_prev + bg
        lin_i[rows, :] = hg
        h_prev = jnp.broadcast_to(hg[SUBLANES - 1:SUBLANES, :], (SUBLANES, d))
    h_carry[...] = h_prev[0:1, :]

    pb1 = _dot(o_sc, wb1_ref[...])
    g_lru = _dot(xn, win_ref[:, c0 + 4 * d:c0 + 5 * d])
    mp = jax.nn.sigmoid(_dot(xn, wm_ref[:, d:2 * d]) + bm_ref[:, d:2 * d]) * pb1
    gate2 = jax.nn.sigmoid(_dot(xn, wm_ref[:, 2 * d:3 * d]) + bm_ref[:, 2 * d:3 * d])
    o_lru = (lin_i[...] * _gelu_tanh(g_lru)).astype(BF16)
    mp_ref[...] = mp + gate2 * _dot(o_lru, wb2_ref[...])

    @pl.when(t == last)
    def _():
        scn_ref[...] = ext_sc[HALO + tm - 2:HALO + tm, :]
        rcn_ref[...] = ext_rc[HALO + tm - 3:HALO + tm, :]
        hl_ref[...] = h_carry[...]

    ext_sc[HALO - 2:HALO, :] = ext_sc[HALO + tm - 2:HALO + tm, :]
    ext_rc[HALO - 3:HALO, :] = ext_rc[HALO + tm - 3:HALO + tm, :]


def _prompt_mix(l, x, w, k_all, v_all, tm):
    bsz, t, d = x.shape
    nt = t // tm
    nmb = tm // MOBA_BLOCK
    n_in = w['w_in'].shape[-1]
    row = lambda b, i: (b, i, 0)
    per_b = lambda b, i: (b, 0, 0)
    vec = lambda n: _const_spec((None, 1, n), (l, 0, 0))
    in_specs = [
        pl.BlockSpec((None, tm, d), row),
        vec(d),
        _const_spec((None, d, n_in), (l, 0, 0)),
        _const_spec((None, 3, d), (l, 0, 0)),
        _const_spec((None, 4, d), (l, 0, 0)),
        vec(d),
        _const_spec((None, LRU_BLOCKS, d // LRU_BLOCKS, 2 * d // LRU_BLOCKS), (l, 0, 0, 0)),
        vec(d),
        vec(d),
        vec(d),
        _const_spec((None, None, d, d), (l, 1, 0, 0)),
        _const_spec((None, None, d, d), (l, 2, 0, 0)),
        _const_spec((None, d, 3 * d), (l, 0, 0)),
        vec(3 * d),
        pl.BlockSpec(memory_space=pl.ANY),
        pl.BlockSpec(memory_space=pl.ANY),
    ]
    out_shape = [
        jax.ShapeDtypeStruct((bsz, N_KV_HEADS, t, d // N_KV_HEADS), BF16),
        jax.ShapeDtypeStruct((bsz, N_KV_HEADS, t, HEAD_DIM), BF16),
        jax.ShapeDtypeStruct((bsz, N_KV_HEADS, HEAD_DIM, t), BF16),
        jax.ShapeDtypeStruct(k_all.shape, F32),
        jax.ShapeDtypeStruct(v_all.shape, F32),
        jax.ShapeDtypeStruct((bsz, nt, nmb, KV_W), F32),
        jax.ShapeDtypeStruct((bsz, t, d), F32),
        jax.ShapeDtypeStruct((bsz, t, d), F32),
        jax.ShapeDtypeStruct((bsz, 2, d), F32),
        jax.ShapeDtypeStruct((bsz, 3, d), F32),
        jax.ShapeDtypeStruct((bsz, 1, d), F32),
    ]
    head_major = lambda b, i: (b, 0, i, 0)
    out_specs = [
        pl.BlockSpec((None, N_KV_HEADS, tm, d // N_KV_HEADS), head_major),
        pl.BlockSpec((None, N_KV_HEADS, tm, HEAD_DIM), head_major),
        pl.BlockSpec((None, N_KV_HEADS, HEAD_DIM, tm), lambda b, i: (b, 0, 0, i)),
        pl.BlockSpec((None, None, tm * N_KV_HEADS, HEAD_DIM), lambda b, i: (l, b, i, 0)),
        pl.BlockSpec((None, None, tm * N_KV_HEADS, HEAD_DIM), lambda b, i: (l, b, i, 0)),
        pl.BlockSpec((None, None, nmb, KV_W), lambda b, i: (b, i, 0, 0)),
        pl.BlockSpec((None, tm, d), row),
        pl.BlockSpec((None, tm, d), row),
        pl.BlockSpec((None, 2, d), per_b),
        pl.BlockSpec((None, 3, d), per_b),
        pl.BlockSpec((None, 1, d), per_b),
    ]
    scratch = [
        pltpu.VMEM((HALO + tm, d), F32),
        pltpu.VMEM((HALO + tm, d), F32),
        pltpu.VMEM((tm, d), F32),
        pltpu.VMEM((tm, d), F32),
        pltpu.VMEM((1, d), F32),
    ]
    return pl.pallas_call(
        functools.partial(_mix_kernel, tm=tm, d=d),
        grid=(bsz, nt), in_specs=in_specs, out_specs=out_specs, out_shape=out_shape,
        scratch_shapes=scratch, name=f"prompt_mix_{l}",
        input_output_aliases={len(in_specs) - 2: 3, len(in_specs) - 1: 4},
        compiler_params=pltpu.CompilerParams(
            dimension_semantics=("arbitrary", "arbitrary"), vmem_limit_bytes=VMEM_LIMIT),
    )(x, w['norm1'], w['w_in'], w['sconv_w'], w['rconv_w'], w['rconv_b'], w['lru_wax'], w['lru_ba'],
      w['lru_bx'], w['lru_lambda'], w['w_branch'], w['w_branch'], w['w_merge'], w['b_merge'], k_all, v_all)


def _attn_kernel(q_ref, k_ref, v_ref, km_ref, o_ref, *, nb):
    blk = MOBA_BLOCK
    nq = GROUP * blk

    parts = []
    rest = km_ref[...]
    for _ in range(3):
        part = rest.astype(BF16).astype(F32)
        parts += [part, jnp.zeros_like(part)]
        rest = rest - part
    km_terms = jnp.concatenate(parts, axis=0).astype(BF16)
    prow = 2 * nb

    block_id = lax.broadcasted_iota(jnp.int32, (nb, nq), 0)
    key = lax.broadcasted_iota(jnp.int32, (blk, nq), 0)
    qpos = lax.broadcasted_iota(jnp.int32, (blk, nq), 1) & (blk - 1)
    causal = key <= qpos

    def scores(i):
        q2 = jnp.concatenate([q_ref[i * blk:(i + 1) * blk, g * HEAD_DIM:(g + 1) * HEAD_DIM]
                              for g in range(GROUP)], axis=0)
        keys = k_ref[0:(i + 1) * blk, :]
        return _dot_nt(keys if i == 0 else jnp.concatenate([keys, km_terms], axis=0), q2)

    s_next = scores(0)
    for i in range(nb):
        q_rows = slice(i * blk, (i + 1) * blk)
        n_keys = (i + 1) * blk
        s_all = s_next
        if i + 1 < nb:
            s_next = scores(i + 1)
        if i > 0:
            gate = (s_all[n_keys + 2 * prow:n_keys + 2 * prow + nb]
                    + s_all[n_keys + prow:n_keys + prow + nb] + s_all[n_keys:n_keys + nb])
            past = block_id < i
            gate = jnp.where(past, gate, -jnp.inf)
            rank = jnp.zeros((nb, nq), jnp.int32)
            for j in range(i):
                gj = gate[j:j + 1, :]
                rank = rank + jnp.where((gj > gate) | ((gj == gate) & (j < block_id)), 1, 0)
            sel = jnp.where(past & (rank < MOBA_TOPK), 1.0, 0.0)

        pieces = [jnp.where(sel[j:j + 1, :] > 0.5, s_all[j * blk:(j + 1) * blk], -jnp.inf) for j in range(i)]
        pieces.append(jnp.where(causal, s_all[i * blk:n_keys], -jnp.inf))
        m = jnp.max(pieces[-1], axis=0, keepdims=True)
        for piece in pieces[:-1]:
            m = jnp.maximum(m, jnp.max(piece, axis=0, keepdims=True))
        probs = [jnp.exp2((piece - m) * (ATT_SCALE * LOG2_E)) for piece in pieces]
        den = jnp.sum(probs[0], axis=0, keepdims=True)
        for pr in probs[1:]:
            den = den + jnp.sum(pr, axis=0, keepdims=True)
        p_all = jnp.concatenate([pr.astype(BF16) for pr in probs], axis=0)
        out = (_dot(v_ref[:, 0:n_keys], p_all) / den).T
        for g in range(GROUP):
            o_ref[q_rows, g * HEAD_DIM:(g + 1) * HEAD_DIM] = out[g * blk:(g + 1) * blk].astype(o_ref.dtype)


def _prompt_attn(q, k, v, kmean):
    bsz, _, t, gw = q.shape
    nb = t // MOBA_BLOCK
    return pl.pallas_call(
        functools.partial(_attn_kernel, nb=nb),
        grid=(bsz, N_KV_HEADS),
        in_specs=[
            pl.BlockSpec((None, None, t, gw), lambda b, h: (b, h, 0, 0)),
            pl.BlockSpec((None, None, t, HEAD_DIM), lambda b, h: (b, h, 0, 0)),
            pl.BlockSpec((None, None, HEAD_DIM, t), lambda b, h: (b, h, 0, 0)),
            pl.BlockSpec((None, nb, HEAD_DIM), lambda b, h: (b, 0, h)),
        ],
        out_specs=pl.BlockSpec((None, None, t, gw), lambda b, h: (b, h, 0, 0)),
        out_shape=jax.ShapeDtypeStruct((bsz, N_KV_HEADS, t, gw), BF16),
        name="prompt_attn",
        compiler_params=pltpu.CompilerParams(
            dimension_semantics=("arbitrary", "arbitrary"), vmem_limit_bytes=VMEM_LIMIT),
    )(q, k, v, kmean)


def _post_kernel(pt_ref, x_ref, att_ref, g0_ref, mp_ref, wb0_ref, wo_ref, g2_ref, wup_ref, fcw_ref, fcb_ref,
                 wdn_ref, gf_ref, cache_ref, y_ref, fcn_ref, km_ref, ext_fc, cbuf, csem,
                 *, layer, tm, d, dff, final, pps):
    t = pl.program_id(1)
    last = pl.num_programs(1) - 1
    step = pl.program_id(0) * pl.num_programs(1) + t
    n_steps = pl.num_programs(0) * pl.num_programs(1)
    slot = step % 2

    def page_copies(s, sl):
        return [pltpu.make_async_copy(cache_ref.at[layer, pt_ref[s * pps + p]], cbuf.at[sl, p], csem.at[sl, p])
                for p in range(pps)]

    @pl.when(step == 0)
    def _():
        for cp in page_copies(step, 0):
            cp.start()

    for cp in page_copies(step, slot):
        cp.wait()

    att = jnp.concatenate([att_ref[h] for h in range(N_KV_HEADS)], axis=1)
    merged = g0_ref[...] * _dot(att, wb0_ref[...]) + mp_ref[...]
    x1 = x_ref[...] + _dot(merged.astype(BF16), wo_ref[...])

    for cp in page_copies(jnp.minimum(step + 1, n_steps - 1), 1 - slot):
        cp.start()
    groups = PAGE_SIZE * N_KV_HEADS // SUBLANES
    for j in range(pps // PAGES_PER_BLOCK):
        tot = jnp.zeros((SUBLANES, HEAD_DIM), F32)
        for p in range(PAGES_PER_BLOCK):
            page = cbuf.at[slot, PAGES_PER_BLOCK * j + p]
            for g in range(groups):
                tot = tot + page[g * SUBLANES:(g + 1) * SUBLANES, :]
        tot = (tot[0:N_KV_HEADS] + tot[N_KV_HEADS:2 * N_KV_HEADS]) * (1.0 / MOBA_BLOCK)
        for h in range(N_KV_HEADS):
            km_ref[h, j:j + 1, :] = tot[h:h + 1, :]

    @pl.when(t == 0)
    def _():
        ext_fc[0:HALO, :] = jnp.zeros((HALO, 2 * dff), F32)

    xn2 = _rms(x1, g2_ref[...]).astype(BF16)
    ext_fc[HALO:HALO + tm, :] = _dot(xn2, wup_ref[...])

    def conv(c0, c1):
        return (fcw_ref[0:1, c0:c1] * ext_fc[HALO - 2:HALO - 2 + tm, c0:c1]
                + fcw_ref[1:2, c0:c1] * ext_fc[HALO - 1:HALO - 1 + tm, c0:c1]
                + fcw_ref[2:3, c0:c1] * ext_fc[HALO:HALO + tm, c0:c1]) + fcb_ref[:, c0:c1]

    hh = (jax.nn.silu(conv(0, dff)) * conv(dff, 2 * dff)).astype(BF16)
    x2 = x1 + _dot(hh, wdn_ref[...])
    y_ref[...] = _rms(x2, gf_ref[...]) if final else x2

    @pl.when(t == last)
    def _():
        fcn_ref[...] = ext_fc[HALO + tm - 2:HALO + tm, :]

    ext_fc[HALO - 2:HALO, :] = ext_fc[HALO + tm - 2:HALO + tm, :]

    @pl.when(step == n_steps - 1)
    def _():
        for cp in page_copies(step, 1 - slot):
            cp.wait()


def _prompt_post(l, x, att, g0, mp, w, cache_k, page_table, tm, final):
    bsz, t, d = x.shape
    dff = w['w_down'].shape[1]
    depth, n_pool, page, hkv, hd = cache_k.shape
    n_b, n_pages = page_table.shape
    n_steps = bsz * (t // tm)
    pps = n_b * n_pages // n_steps
    assert pps * n_steps == n_b * n_pages and n_pages % pps == 0
    assert pps % (PAGES_PER_BLOCK * SUBLANES) == 0
    bps = pps // PAGES_PER_BLOCK
    nt = t // tm
    row = lambda b, i, pt: (b, i, 0)
    tile = pl.BlockSpec((None, tm, d), row)
    in_specs = [
        tile, pl.BlockSpec((None, N_KV_HEADS, tm, d // N_KV_HEADS), lambda b, i, pt: (b, 0, i, 0)), tile, tile,
        _const_spec((None, None, d, d), (l, 0, 0, 0)),
        _const_spec((None, d, d), (l, 0, 0)),
        _const_spec((None, 1, d), (l, 0, 0)),
        _const_spec((None, d, 2 * dff), (l, 0, 0)),
        _const_spec((None, 3, 2 * dff), (l, 0, 0)),
        _const_spec((None, 1, 2 * dff), (l, 0, 0)),
        _const_spec((None, dff, d), (l, 0, 0)),
        _const_spec((1, d), (0, 0)),
        pl.BlockSpec(memory_space=pl.ANY),
    ]
    grid_spec = pltpu.PrefetchScalarGridSpec(
        num_scalar_prefetch=1,
        grid=(bsz, nt),
        in_specs=in_specs,
        out_specs=[tile,
                   pl.BlockSpec((None, 2, 2 * dff), lambda b, i, pt: (b, 0, 0)),
                   pl.BlockSpec((None, hkv, bps, hd), lambda b, i, pt: (b * nt + i, 0, 0, 0))],
        scratch_shapes=[pltpu.VMEM((HALO + tm, 2 * dff), F32),
                        pltpu.VMEM((2, pps, page * hkv, hd), F32),
                        pltpu.SemaphoreType.DMA((2, pps))],
    )
    y, fcn, km = pl.pallas_call(
        functools.partial(_post_kernel, layer=l, tm=tm, d=d, dff=dff, final=final, pps=pps),
        grid_spec=grid_spec,
        out_shape=[jax.ShapeDtypeStruct((bsz, t, d), F32),
                   jax.ShapeDtypeStruct((bsz, 2, 2 * dff), F32),
                   jax.ShapeDtypeStruct((n_steps, hkv, bps, hd), F32)],
        name=f"prompt_post_{l}",
        compiler_params=pltpu.CompilerParams(
            dimension_semantics=("arbitrary", "arbitrary"), vmem_limit_bytes=POST_VMEM_LIMIT),
    )(page_table.reshape(-1), x, att, g0, mp, w['w_branch'], w['w_o'], w['norm2'], w['w_up'], w['fconv_w'],
      w['fconv_b'], w['w_down'], w['norm_f'], cache_k.reshape(depth, n_pool, page * hkv, hd))
    return y, fcn, km.reshape(n_b, n_pages // pps, hkv, bps, hd)


def _smix_kernel(x_ref, g1_ref, win_ref, scw_ref, rcw_ref, rcb_ref, wa_ref, ba_ref, wx_ref, bx_ref,
                 lam_ref, wb1_ref, wb2_ref, wm_ref, bm_ref, sst_ref, rst_ref, h0_ref,
                 q_ref, k_ref, v_ref, g0_ref, mp_ref, scn_ref, rcn_ref, hl_ref, *, d):
    xn = _rms(x_ref[...], g1_ref[...]).astype(BF16)
    q_ref[...] = _dot(xn, win_ref[:, 0:d])
    k_ref[...] = _dot(xn, win_ref[:, d:d + KV_W])
    v_ref[...] = _dot(xn, win_ref[:, d + KV_W:d + 2 * KV_W])
    c0 = d + 2 * KV_W

    u = _dot(xn, win_ref[:, c0 + d:c0 + 2 * d]) * _dot(xn, win_ref[:, c0 + 2 * d:c0 + 3 * d])
    sc = scw_ref[0:1, :] * sst_ref[0] + scw_ref[1:2, :] * sst_ref[1] + scw_ref[2:3, :] * u
    scn_ref[0] = sst_ref[1]
    scn_ref[1] = u
    o_sc = (_dot(xn, win_ref[:, c0:c0 + d]) * sc).astype(BF16)
    pb1 = _dot(o_sc, wb1_ref[...])

    x_lru = _dot(xn, win_ref[:, c0 + 3 * d:c0 + 4 * d])
    xc = (rcw_ref[0:1, :] * rst_ref[0] + rcw_ref[1:2, :] * rst_ref[1] + rcw_ref[2:3, :] * rst_ref[2]
          + rcw_ref[3:4, :] * x_lru) + rcb_ref[...]
    rcn_ref[0] = rst_ref[1]
    rcn_ref[1] = rst_ref[2]
    rcn_ref[2] = x_lru
    bw = d // LRU_BLOCKS
    xcb = xc.astype(BF16)
    r_lin = jnp.concatenate(
        [_dot(xcb[:, n * bw:(n + 1) * bw], wa_ref[n]) for n in range(LRU_BLOCKS)], axis=1)
    i_lin = jnp.concatenate(
        [_dot(xcb[:, n * bw:(n + 1) * bw], wx_ref[n]) for n in range(LRU_BLOCKS)], axis=1)
    a, b = _lru_coeffs(xc, r_lin + ba_ref[...], i_lin + bx_ref[...], lam_ref[...])
    h = a * h0_ref[...] + b
    hl_ref[...] = h
    o_lru = (h * _gelu_tanh(_dot(xn, win_ref[:, c0 + 4 * d:c0 + 5 * d]))).astype(BF16)
    pb2 = _dot(o_lru, wb2_ref[...])

    g0_ref[...] = jax.nn.sigmoid(_dot(xn, wm_ref[:, 0:d]) + bm_ref[:, 0:d])
    mp_ref[...] = (jax.nn.sigmoid(_dot(xn, wm_ref[:, d:2 * d]) + bm_ref[:, d:2 * d]) * pb1
                   + jax.nn.sigmoid(_dot(xn, wm_ref[:, 2 * d:3 * d]) + bm_ref[:, 2 * d:3 * d]) * pb2)


def _sample_mix(l, x, sst, rst, h0, w):
    n, d = x.shape
    n_in = w['w_in'].shape[-1]
    full = lambda shape: pl.BlockSpec(shape, lambda i: (0,) * len(shape))
    lay = lambda shape, idx: pl.BlockSpec(shape, lambda i: idx)
    vec = lambda m: lay((None, 1, m), (l, 0, 0))
    lw = (None, LRU_BLOCKS, d // LRU_BLOCKS, d // LRU_BLOCKS)
    in_specs = [
        full((n, d)), vec(d), lay((None, d, n_in), (l, 0, 0)), lay((None, 3, d), (l, 0, 0)),
        lay((None, 4, d), (l, 0, 0)), vec(d), lay(lw, (l, 0, 0, 0)), vec(d), lay(lw, (l, 0, 0, 0)),
        vec(d), vec(d), lay((None, None, d, d), (l, 1, 0, 0)), lay((None, None, d, d), (l, 2, 0, 0)),
        lay((None, d, 3 * d), (l, 0, 0)), vec(3 * d),
        lay((None, 2, n, d), (l, 0, 0, 0)), lay((None, 3, n, d), (l, 0, 0, 0)), lay((None, n, d), (l, 0, 0)),
    ]
    out_shape = [
        jax.ShapeDtypeStruct((n, d), F32), jax.ShapeDtypeStruct((n, KV_W), F32),
        jax.ShapeDtypeStruct((n, KV_W), F32), jax.ShapeDtypeStruct((n, d), F32),
        jax.ShapeDtypeStruct((n, d), F32), jax.ShapeDtypeStruct((2, n, d), F32),
        jax.ShapeDtypeStruct((3, n, d), F32), jax.ShapeDtypeStruct((n, d), F32),
    ]
    out_specs = [full(s.shape) for s in out_shape]
    return pl.pallas_call(
        functools.partial(_smix_kernel, d=d),
        grid=(1,), in_specs=in_specs, out_specs=out_specs, out_shape=out_shape,
        name=f"sample_mix_{l}",
        compiler_params=pltpu.CompilerParams(
            dimension_semantics=("arbitrary",), vmem_limit_bytes=VMEM_LIMIT),
    )(x, w['norm1'], w['w_in'], w['sconv_w'], w['rconv_w'], w['rconv_b'], w['lru_wa'], w['lru_ba'],
      w['lru_wx'], w['lru_bx'], w['lru_lambda'], w['w_branch'], w['w_branch'], w['w_merge'],
      w['b_merge'], sst, rst, h0)


def _spost_kernel(x_ref, att_ref, g0_ref, mp_ref, wb0_ref, wo_ref, g2_ref, wup_ref, fcw_ref, fcb_ref,
                  wdn_ref, gf_ref, fst_ref, y_ref, fcn_ref, *, dff, final):
    merged = g0_ref[...] * _dot(att_ref[...].astype(BF16), wb0_ref[...]) + mp_ref[...]
    x1 = x_ref[...] + _dot(merged.astype(BF16), wo_ref[...])
    xn2 = _rms(x1, g2_ref[...]).astype(BF16)
    up = _dot(xn2, wup_ref[...])
    c = (fcw_ref[0:1, :] * fst_ref[0] + fcw_ref[1:2, :] * fst_ref[1] + fcw_ref[2:3, :] * up) + fcb_ref[...]
    fcn_ref[0] = fst_ref[1]
    fcn_ref[1] = up
    hh = (jax.nn.silu(c[:, 0:dff]) * c[:, dff:2 * dff]).astype(BF16)
    x2 = x1 + _dot(hh, wdn_ref[...])
    y_ref[...] = _rms(x2, gf_ref[...]) if final else x2


def _sample_post(l, x, att, g0, mp, fst, w, final):
    n, d = x.shape
    dff = w['w_down'].shape[1]
    full = lambda shape: pl.BlockSpec(shape, lambda i: (0,) * len(shape))
    lay = lambda shape, idx: pl.BlockSpec(shape, lambda i: idx)
    in_specs = [
        full((n, d)), full((n, d)), full((n, d)), full((n, d)),
        lay((None, None, d, d), (l, 0, 0, 0)), lay((None, d, d), (l, 0, 0)), lay((None, 1, d), (l, 0, 0)),
        lay((None, d, 2 * dff), (l, 0, 0)), lay((None, 3, 2 * dff), (l, 0, 0)),
        lay((None, 1, 2 * dff), (l, 0, 0)), lay((None, dff, d), (l, 0, 0)), full((1, d)),
        lay((None, 2, n, 2 * dff), (l, 0, 0, 0)),
    ]
    out_shape = [jax.ShapeDtypeStruct((n, d), F32), jax.ShapeDtypeStruct((2, n, 2 * dff), F32)]
    return pl.pallas_call(
        functools.partial(_spost_kernel, dff=dff, final=final),
        grid=(1,), in_specs=in_specs, out_specs=[full(s.shape) for s in out_shape], out_shape=out_shape,
        name=f"sample_post_{l}",
        compiler_params=pltpu.CompilerParams(
            dimension_semantics=("arbitrary",), vmem_limit_bytes=VMEM_LIMIT),
    )(x, att, g0, mp, w['w_branch'], w['w_o'], w['norm2'], w['w_up'], w['fconv_w'], w['fconv_b'],
      w['w_down'], w['norm_f'], fst)


CHOOSE_SEQS = 8


def _choose_kernel(q_ref, km_ref, o_ref, *, nb):
    lane = lax.broadcasted_iota(jnp.int32, (nb, HEAD_DIM), 1)
    blk = lax.broadcasted_iota(jnp.int32, (nb, HEAD_DIM), 0)
    row = lax.broadcasted_iota(jnp.int32, (SUBLANES, HEAD_DIM), 0)
    for s in range(q_ref.shape[0]):
        gate = jnp.zeros((nb, HEAD_DIM), F32)
        for h in range(N_HEADS):
            gh = jnp.concatenate(
                [jnp.sum(km_ref[s, part, h // GROUP] * q_ref[s, h:h + 1, :], axis=1, keepdims=True)
                 for part in range(km_ref.shape[1])], axis=0)
            gate = jnp.where(lane == h, gh, gate)
        out = jnp.zeros((SUBLANES, HEAD_DIM), jnp.int32)
        for r in range(MOBA_TOPK):
            best = jnp.max(gate, axis=0, keepdims=True)
            idx = jnp.min(jnp.where(gate == best, blk, nb), axis=0, keepdims=True)
            out = jnp.where(row == r, idx, out)
            gate = jnp.where(blk == idx, -jnp.inf, gate)
        o_ref[s] = out


def _sample_choose(q, kmean_l):
    n = q.shape[0]
    parts, bps = kmean_l.shape[1], kmean_l.shape[3]
    ns = math.gcd(n, CHOOSE_SEQS)
    return pl.pallas_call(
        functools.partial(_choose_kernel, nb=parts * bps),
        grid=(n // ns,),
        in_specs=[pl.BlockSpec((ns, N_HEADS, HEAD_DIM), lambda b: (b, 0, 0)),
                  pl.BlockSpec((ns, parts, N_KV_HEADS, bps, HEAD_DIM), lambda b: (b, 0, 0, 0, 0))],
        out_specs=pl.BlockSpec((ns, SUBLANES, HEAD_DIM), lambda b: (b, 0, 0)),
        out_shape=jax.ShapeDtypeStruct((n, SUBLANES, HEAD_DIM), jnp.int32),
        name="sample_choose",
        compiler_params=pltpu.CompilerParams(dimension_semantics=("arbitrary",)),
    )(q.reshape(n, N_HEADS, HEAD_DIM), kmean_l)


def _sattn_kernel(sel_ref, pt_ref, q_ref, kn_ref, vn_ref, ck_ref, cv_ref, o_ref, kbuf, vbuf, sem,
                  *, l, n_pages):
    b = pl.program_id(0)
    n = pl.num_programs(0)
    slot = b % 2

    def copies(seq, sl):
        out = []
        for h in range(N_HEADS):
            kv = h // GROUP
            for r in range(MOBA_TOPK):
                blk = sel_ref[(seq * MOBA_TOPK + r) * N_HEADS + h]
                for p in range(PAGES_PER_BLOCK):
                    page = pt_ref[seq * n_pages + blk * PAGES_PER_BLOCK + p]
                    part = r * PAGES_PER_BLOCK + p
                    for c, (src, dst) in enumerate(((ck_ref, kbuf), (cv_ref, vbuf))):
                        out.append(pltpu.make_async_copy(
                            src.at[l, page, :, kv, :],
                            dst.at[sl, h, pl.ds(part * PAGE_SIZE, PAGE_SIZE), :],
                            sem.at[sl, c, h, part]))
        return out

    @pl.when(b == 0)
    def _():
        for j, cp in enumerate(copies(b, 0)):
            cp.start(priority=j % 2)

    for cp in copies(b, slot):
        cp.wait()
    for j, cp in enumerate(copies(jnp.minimum(b + 1, n - 1), 1 - slot)):
        cp.start(priority=j % 2)

    for h in range(N_HEADS):
        kv = h // GROUP
        q = q_ref[h:h + 1, :]
        s = jnp.sum(kbuf[slot, h] * q, axis=1, keepdims=True) * ATT_SCALE
        s_new = jnp.sum(kn_ref[kv:kv + 1, :] * q, axis=1, keepdims=True) * ATT_SCALE
        m = jnp.maximum(jnp.max(s, axis=0, keepdims=True), s_new)
        p = jnp.exp(s - m)
        p_new = jnp.exp(s_new - m)
        den = jnp.sum(p, axis=0, keepdims=True) + p_new
        num = jnp.sum(p * vbuf[slot, h], axis=0, keepdims=True) + p_new * vn_ref[kv:kv + 1, :]
        o_ref[h:h + 1, :] = num / den

    @pl.when(b == n - 1)
    def _():
        for cp in copies(b, 1 - slot):
            cp.wait()


def _sample_attn(l, sel, page_table, q, k_new, v_new, cache_k, cache_v):
    n = q.shape[0]
    n_pages = page_table.shape[1]
    rows = MOBA_TOPK * MOBA_BLOCK
    grid_spec = pltpu.PrefetchScalarGridSpec(
        num_scalar_prefetch=2,
        grid=(n,),
        in_specs=[pl.BlockSpec((None, N_HEADS, HEAD_DIM), lambda b, *_: (b, 0, 0)),
                  pl.BlockSpec((None, N_KV_HEADS, HEAD_DIM), lambda b, *_: (b, 0, 0)),
                  pl.BlockSpec((None, N_KV_HEADS, HEAD_DIM), lambda b, *_: (b, 0, 0)),
                  pl.BlockSpec(memory_space=pl.ANY),
                  pl.BlockSpec(memory_space=pl.ANY)],
        out_specs=pl.BlockSpec((None, N_HEADS, HEAD_DIM), lambda b, *_: (b, 0, 0)),
        scratch_shapes=[pltpu.VMEM((2, N_HEADS, rows, HEAD_DIM), F32),
                        pltpu.VMEM((2, N_HEADS, rows, HEAD_DIM), F32),
                        pltpu.SemaphoreType.DMA((2, 2, N_HEADS, MOBA_TOPK * PAGES_PER_BLOCK))],
    )
    sel_flat = sel[:, :MOBA_TOPK, :N_HEADS].reshape(-1)
    out = pl.pallas_call(
        functools.partial(_sattn_kernel, l=l, n_pages=n_pages),
        grid_spec=grid_spec,
        out_shape=jax.ShapeDtypeStruct((n, N_HEADS, HEAD_DIM), F32),
        name=f"sample_attn_{l}",
        compiler_params=pltpu.CompilerParams(dimension_semantics=("arbitrary",)),
    )(sel_flat, page_table.reshape(-1), q.reshape(n, N_HEADS, HEAD_DIM),
      k_new.reshape(n, N_KV_HEADS, HEAD_DIM), v_new.reshape(n, N_KV_HEADS, HEAD_DIM), cache_k, cache_v)
    return out.reshape(n, N_HEADS * HEAD_DIM)


def _trunk(x_prompt, x_sample, cache_k, cache_v, state_sconv, state_rconv, state_lru, state_fconv,
           page_table, w, tm):
    depth = w['w_in'].shape[0]
    bsz, t, d = x_prompt.shape
    n = x_sample.shape[0]
    xp = x_prompt
    xs = x_sample.reshape(n, d)
    sst = jnp.swapaxes(state_sconv, 1, 2)
    rst = jnp.swapaxes(state_rconv, 1, 2)
    fst = jnp.swapaxes(state_fconv, 1, 2)

    k_all = jnp.zeros((depth, bsz, t * N_KV_HEADS, HEAD_DIM), F32)
    v_all = jnp.zeros((depth, bsz, t * N_KV_HEADS, HEAD_DIM), F32)
    st_p, st_s = [], []
    for l in range(depth):
        final = l == depth - 1
        q, kb, vb, k_all, v_all, km, g0, mp, scn, rcn, hl = _prompt_mix(l, xp, w, k_all, v_all, tm)
        att = _prompt_attn(q, kb, vb, km.reshape(bsz, t // MOBA_BLOCK, KV_W))
        xp, fcn, kmean_s = _prompt_post(l, xp, att, g0, mp, w, cache_k, page_table, tm, final)
        st_p.append((scn, rcn, hl.reshape(bsz, d), fcn))

        qs, ks, vs, g0s, mps, scns, rcns, hls = _sample_mix(l, xs, sst, rst, state_lru, w)
        sel = _sample_choose(qs, kmean_s)
        atts = _sample_attn(l, sel, page_table, qs, ks, vs, cache_k, cache_v)
        xs, fcns = _sample_post(l, xs, atts, g0s, mps, fst, w, final)
        st_s.append((ks.reshape(n, 1, N_KV_HEADS, HEAD_DIM), vs.reshape(n, 1, N_KV_HEADS, HEAD_DIM),
                     jnp.swapaxes(scns, 0, 1), jnp.swapaxes(rcns, 0, 1), hls, jnp.swapaxes(fcns, 0, 1)))

    outs_p = [k_all.reshape(depth, bsz, t, N_KV_HEADS, HEAD_DIM), v_all.reshape(depth, bsz, t, N_KV_HEADS, HEAD_DIM)]
    outs_p += [jnp.stack([s[i] for s in st_p], axis=0) for i in range(4)]
    outs_s = [jnp.stack([s[i] for s in st_s], axis=0) for i in range(6)]
    return (xp, xs.reshape(n, 1, d), *outs_p, *outs_s)


def kernel(x_prompt, x_sample, cache_k, cache_v, state_sconv, state_rconv, state_lru, state_fconv, page_table, norm1, w_in, sconv_w, rconv_w, rconv_b, lru_wa, lru_ba, lru_wx, lru_bx, lru_lambda, w_branch, w_merge, b_merge, w_o, norm2, w_up, fconv_w, fconv_b, w_down, norm_f):
    row = lambda a: a.reshape(a.shape[0], 1, a.shape[1])
    w = dict(
        norm1=row(norm1), w_in=w_in.astype(BF16), sconv_w=sconv_w, rconv_w=rconv_w, rconv_b=row(rconv_b),
        lru_wa=lru_wa.astype(BF16), lru_ba=row(lru_ba), lru_wx=lru_wx.astype(BF16), lru_bx=row(lru_bx),
        lru_wax=jnp.concatenate([lru_wa, lru_wx], axis=-1).astype(BF16),
        lru_lambda=row(lru_lambda), w_branch=w_branch.astype(BF16), w_merge=w_merge.astype(BF16),
        b_merge=row(b_merge), w_o=w_o.astype(BF16), norm2=row(norm2), w_up=w_up.astype(BF16),
        fconv_w=fconv_w, fconv_b=row(fconv_b), w_down=w_down.astype(BF16), norm_f=norm_f.reshape(1, -1),
    )
    return _trunk(x_prompt, x_sample, cache_k, cache_v, state_sconv, state_rconv, state_lru, state_fconv,
                  page_table, w, TIME_TILE)
```
